```python
import math
import jax, jax.numpy as jnp
from jax import lax
import numpy as np

D_MODEL = 2048
BATCH = 1
SEQ = 8192
DEPTH = 2

HEAD_DIM = 128
SB_HEADS = 8
FOX_HEADS = 8
SB_WIDTH = SB_HEADS * HEAD_DIM
FOX_WIDTH = FOX_HEADS * HEAD_DIM
N_META = 16
BLOCK_Q = 128
N_PAD = BLOCK_Q - N_META
PREFIX = N_PAD + N_META
N_EXPERTS = 16
N_GROUPS = 4
EXPERTS_PER_GROUP = N_EXPERTS // N_GROUPS
TOP_K = 2
EXPERT_FF = 1024
MOE_BLOCK = 256
ALPHA = (2 * DEPTH) ** 0.25
BETA = (8 * DEPTH) ** -0.25
LN_EPS = 1e-5
NEG = -1e30

_Q_SB = SB_WIDTH
_K_SB = 2 * SB_WIDTH
_V_SB = 3 * SB_WIDTH
_Q_FX = _V_SB + FOX_WIDTH
_K_FX = _V_SB + 2 * FOX_WIDTH
_V_FX = _V_SB + 3 * FOX_WIDTH
_F_FX = _V_FX + FOX_HEADS
_G_SB = _F_FX + D_MODEL
IN_COLS = _G_SB + D_MODEL
SPLITS = (_Q_SB, _K_SB, _V_SB, _Q_FX, _K_FX, _V_FX, _F_FX, _G_SB)

kernel_name = "stickbreak_fox_grouped_moe_hybrid"


def layer_norm(x, g, b):
    xf = x.astype(jnp.float32)
    mu = jnp.mean(xf, axis=-1, keepdims=True)
    var = jnp.mean(jnp.square(xf - mu), axis=-1, keepdims=True)
    y = (xf - mu) * lax.rsqrt(var + LN_EPS) * g.astype(jnp.float32) + b.astype(jnp.float32)
    return y.astype(x.dtype)


def split_heads(t, n_heads):
    b, l, _ = t.shape
    return t.reshape(b, l, n_heads, HEAD_DIM).transpose(0, 2, 1, 3)


def merge_heads(t):
    b, h, l, d = t.shape
    return t.transpose(0, 2, 1, 3).reshape(b, l, h * d)


def stick_breaking_attention(q, k, v, key_valid):
    b, h, l, dh = q.shape
    scale = dh ** -0.5
    kpos = jnp.arange(l)

    def block(i):
        qb = lax.dynamic_slice_in_dim(q, i * BLOCK_Q, BLOCK_Q, axis=2)
        qpos = i * BLOCK_Q + jnp.arange(BLOCK_Q)
        vis = (kpos[None, :] < qpos[:, None]) & key_valid[None, :]
        z = jnp.einsum('bhqd,bhkd->bhqk', qb, k, preferred_element_type=jnp.float32) * scale
        log_beta = jax.nn.log_sigmoid(z)
        log_keep = jnp.where(vis, jax.nn.log_sigmoid(-z), 0.0)
        suffix = lax.cumsum(log_keep, axis=3, reverse=True)
        after = jnp.concatenate([suffix[..., 1:], jnp.zeros_like(suffix[..., :1])], axis=3)
        w = jnp.where(vis, jnp.exp(log_beta + after), 0.0)
        return jnp.einsum('bhqk,bhkd->bhqd', w.astype(v.dtype), v)

    out = lax.map(block, jnp.arange(l // BLOCK_Q))
    return out.transpose(1, 2, 0, 3, 4).reshape(b, h, l, dh)


def forgetting_attention(q, k, v, log_f_cum, key_valid):
    b, h, l, dh = q.shape
    scale = dh ** -0.5
    kpos = jnp.arange(l)

    def block(i):
        qb = lax.dynamic_slice_in_dim(q, i * BLOCK_Q, BLOCK_Q, axis=2)
        cq = lax.dynamic_slice_in_dim(log_f_cum, i * BLOCK_Q, BLOCK_Q, axis=2)
        qpos = i * BLOCK_Q + jnp.arange(BLOCK_Q)
        vis = (kpos[None, :] <= qpos[:, None]) & key_valid[None, :]
        s = jnp.einsum('bhqd,bhkd->bhqk', qb, k, preferred_element_type=jnp.float32) * scale
        s = s + cq[..., :, None] - log_f_cum[..., None, :]
        p = jax.nn.softmax(jnp.where(vis, s, NEG), axis=-1)
        return jnp.einsum('bhqk,bhkd->bhqd', p.astype(v.dtype), v)

    out = lax.map(block, jnp.arange(l // BLOCK_Q))
    return out.transpose(1, 2, 0, 3, 4).reshape(b, h, l, dh)


def hybrid_mixer(h, w_in, b_forget, w_branch_sb, w_branch_fox, w_out, key_valid):
    proj = h @ w_in
    q_sb, k_sb, v_sb, q_fx, k_fx, v_fx, f_logit, g_sb, g_fx = jnp.split(proj, SPLITS, axis=-1)
    log_f = jax.nn.log_sigmoid((f_logit + b_forget).astype(jnp.float32))
    log_f = jnp.where(key_valid[None, :, None], log_f, 0.0)
    log_f_cum = jnp.cumsum(log_f, axis=1).transpose(0, 2, 1)
    y_sb = stick_breaking_attention(split_heads(q_sb, SB_HEADS), split_heads(k_sb, SB_HEADS),
                                    split_heads(v_sb, SB_HEADS), key_valid)
    y_fx = forgetting_attention(split_heads(q_fx, FOX_HEADS), split_heads(k_fx, FOX_HEADS),
                                split_heads(v_fx, FOX_HEADS), log_f_cum, key_valid)
    merged = (jax.nn.sigmoid(g_sb) * (merge_heads(y_sb) @ w_branch_sb)
              + jax.nn.sigmoid(g_fx) * (merge_heads(y_fx) @ w_branch_fox))
    return merged @ w_out


def grouped_top2_route(hf, w_router, router_bias):
    n = hf.shape[0]
    aff = jax.nn.sigmoid(jnp.dot(hf, w_router, preferred_element_type=jnp.float32))
    sel = aff + router_bias.astype(jnp.float32)
    grp_score = lax.top_k(sel.reshape(n, N_GROUPS, EXPERTS_PER_GROUP), TOP_K)[0].sum(-1)
    best_group = jnp.argmax(grp_score, axis=-1)
    in_group = (jnp.arange(N_EXPERTS) // EXPERTS_PER_GROUP)[None, :] == best_group[:, None]
    _, idx = lax.top_k(jnp.where(in_group, sel, -jnp.inf), TOP_K)
    w = jnp.take_along_axis(aff, idx, axis=-1)
    return idx, w / jnp.sum(w, axis=-1, keepdims=True)


def moe_ffn(h, w_router, router_bias, w_gate, w_up, w_down):
    b, l, d = h.shape
    n = b * l
    hf = h.reshape(n, d)
    idx, wts = grouped_top2_route(hf, w_router, router_bias)
    n_slots = n * TOP_K
    slot_e = idx.reshape(-1)
    slot_tok = jnp.repeat(jnp.arange(n, dtype=jnp.int32), TOP_K)
    slot_w = wts.reshape(-1)
    order = jnp.argsort(slot_e)
    e_sorted = slot_e[order]
    counts = jnp.bincount(slot_e, length=N_EXPERTS)
    padded = (counts + MOE_BLOCK - 1) // MOE_BLOCK * MOE_BLOCK
    start = jnp.cumsum(counts) - counts
    pend = jnp.cumsum(padded)
    pstart = pend - padded
    dest = pstart[e_sorted] + jnp.arange(n_slots) - start[e_sorted]
    n_blocks = -(-(n_slots + N_EXPERTS * (MOE_BLOCK - 1)) // MOE_BLOCK)
    p = n_blocks * MOE_BLOCK
    buf_tok = jnp.zeros((p,), jnp.int32).at[dest].set(slot_tok[order])
    buf_w = jnp.zeros((p,), h.dtype).at[dest].set(slot_w[order].astype(h.dtype))
    block_e = jnp.minimum(jnp.searchsorted(pend, jnp.arange(n_blocks) * MOE_BLOCK, side='right'),
                          N_EXPERTS - 1)
    xin = hf[buf_tok].reshape(n_blocks, MOE_BLOCK, d)

    def expert_block(args):
        xb, e = args
        return (jax.nn.silu(xb @ w_gate[e]) * (xb @ w_up[e])) @ w_down[e]

    yb = lax.map(expert_block, (xin, block_e)).reshape(p, d)
    y = jnp.zeros((n, d), h.dtype).at[buf_tok].add(yb * buf_w[:, None])
    return y.reshape(b, l, d)


def setup_inputs(seed: int = 0) -> dict:
    key = jax.random.key(seed)
    ks = jax.random.split(key, 20)
    f32 = jnp.float32
    d = D_MODEL
    col_scale = jnp.concatenate([
        jnp.ones((2 * SB_WIDTH,), f32), jnp.full((SB_WIDTH,), BETA, f32),
        jnp.ones((2 * FOX_WIDTH,), f32), jnp.full((FOX_WIDTH,), BETA, f32),
        jnp.ones((FOX_HEADS + 2 * d,), f32)])
    return {
        "x": jax.random.normal(ks[0], (BATCH, SEQ, d), f32),
        "meta_tokens": jax.random.normal(ks[1], (N_META, d), f32),
        "ln_in_g": 1.0 + 0.02 * jax.random.normal(ks[2], (d,), f32),
        "ln_in_b": 0.02 * jax.random.normal(ks[3], (d,), f32),
        "w_in": jax.random.normal(ks[4], (DEPTH, d, IN_COLS), f32) * (d ** -0.5) * col_scale,
        "b_forget": jax.random.uniform(ks[5], (DEPTH, FOX_HEADS), f32, 1.0, 4.0),
        "w_branch_sb": jax.random.normal(ks[6], (DEPTH, SB_WIDTH, d), f32) * SB_WIDTH ** -0.5,
        "w_branch_fox": jax.random.normal(ks[7], (DEPTH, FOX_WIDTH, d), f32) * FOX_WIDTH ** -0.5,
        "w_out": jax.random.normal(ks[8], (DEPTH, d, d), f32) * (d ** -0.5) * BETA,
        "ln_mix_g": 1.0 + 0.02 * jax.random.normal(ks[9], (DEPTH, d), f32),
        "ln_mix_b": 0.02 * jax.random.normal(ks[10], (DEPTH, d), f32),
        "w_router": jax.random.normal(ks[11], (d, N_EXPERTS), f32) * d ** -0.5,
        "router_bias": 0.01 * jax.random.normal(ks[12], (N_EXPERTS,), f32),
        "w_gate": jax.random.normal(ks[13], (DEPTH, N_EXPERTS, d, EXPERT_FF), f32) * d ** -0.5,
        "w_up": jax.random.normal(ks[14], (DEPTH, N_EXPERTS, d, EXPERT_FF), f32) * d ** -0.5,
        "w_down": jax.random.normal(ks[15], (DEPTH, N_EXPERTS, EXPERT_FF, d), f32) * (EXPERT_FF ** -0.5) * BETA,
        "ln_ffn_g": 1.0 + 0.02 * jax.random.normal(ks[16], (DEPTH, d), f32),
        "ln_ffn_b": 0.02 * jax.random.normal(ks[17], (DEPTH, d), f32),
    }


def reference(x, meta_tokens, ln_in_g, ln_in_b, w_in, b_forget, w_branch_sb, w_branch_fox, w_out,
              ln_mix_g, ln_mix_b, w_router, router_bias, w_gate, w_up, w_down, ln_ffn_g, ln_ffn_b):
    b, s, d = x.shape
    l = s + PREFIX
    pad = jnp.zeros((b, N_PAD, d), x.dtype)
    meta = jnp.broadcast_to(meta_tokens.astype(x.dtype)[None], (b, N_META, d))
    h = jnp.concatenate([pad, meta, x], axis=1)
    key_valid = jnp.arange(l) >= N_PAD
    h = layer_norm(h, ln_in_g, ln_in_b)
    for i in range(DEPTH):
        mix = hybrid_mixer(h, w_in[i], b_forget[i], w_branch_sb[i], w_branch_fox[i], w_out[i], key_valid)
        h = layer_norm(ALPHA * h + mix, ln_mix_g[i], ln_mix_b[i])
        ffn = moe_ffn(h, w_router, router_bias, w_gate[i], w_up[i], w_down[i])
        h = layer_norm(ALPHA * h + ffn, ln_ffn_g[i], ln_ffn_b[i])
    return h[:, PREFIX:, :]
```

```python
import functools

import jax
import jax.numpy as jnp
from jax import lax
from jax.experimental import pallas as pl
from jax.experimental.pallas import tpu as pltpu

F32 = jnp.float32
BF16 = jnp.bfloat16

D_MODEL = 2048
DEPTH = 2
HEAD_DIM = 128
N_HEADS = 8
WIDTH = N_HEADS * HEAD_DIM
N_META = 16
N_PAD = 112
PREFIX = N_PAD + N_META
N_EXPERTS = 16
N_GROUPS = 4
EXPERTS_PER_GROUP = N_EXPERTS // N_GROUPS
TOP_K = 2
EXPERT_FF = 1024
ALPHA = (2 * DEPTH) ** 0.25
LN_EPS = 1e-5
NEG = -1e30
QK_SCALE = HEAD_DIM ** -0.5
QKV_COLS = 6 * WIDTH
F_COL0 = QKV_COLS
G_COL0 = QKV_COLS + N_HEADS

SB_EXIT = -104.0

LANE = 128
VMEM_LIMIT = 56 * 1024 * 1024
TM_PROJ = 1664
TN_PROJ = 512
TM_LN = 640
TK_OUT = 512
SB_TQ = 128
SB_TK = 128
FOX_T = 640
MOE_TM = 512
MOE_TF = 256
CMB_TM = 320


def _cparams(sem):
    return pltpu.CompilerParams(dimension_semantics=sem, vmem_limit_bytes=VMEM_LIMIT)


def _layer_norm(x, g, b):
    mu = jnp.mean(x, axis=-1, keepdims=True)
    xc = x - mu
    var = jnp.mean(xc * xc, axis=-1, keepdims=True)
    return xc * lax.rsqrt(var + LN_EPS) * g + b


def _log_sigmoid(z):
    return jnp.minimum(z, 0.0) - jnp.log(1.0 + jnp.exp(-jnp.abs(z)))


def _ln_in_kernel(x_ref, g_ref, b_ref, of_ref, ob_ref):
    y = _layer_norm(x_ref[...], g_ref[...], b_ref[...])
    of_ref[...] = y
    ob_ref[...] = y.astype(BF16)


def _ln_in(x, g, b):
    l, d = x.shape
    row = pl.BlockSpec((TM_LN, d), lambda i: (i, 0))
    vec = pl.BlockSpec((1, d), lambda i: (0, 0))
    return pl.pallas_call(
        _ln_in_kernel,
        grid=(l // TM_LN,),
        in_specs=[row, vec, vec],
        out_specs=[row, row],
        out_shape=[jax.ShapeDtypeStruct((l, d), F32), jax.ShapeDtypeStruct((l, d), BF16)],
        compiler_params=_cparams(("arbitrary",)),
        name="ln_in",
    )(x, g.reshape(1, d), b.reshape(1, d))


def _qkv_kernel(x_ref, w_ref, o_ref, wb_ref):
    j = pl.program_id(0)
    i = pl.program_id(1)

    @pl.when(i == 0)
    def _():
        wb_ref[...] = w_ref[...].astype(BF16)

    acc = jnp.dot(x_ref[...], wb_ref[...], preferred_element_type=F32)
    part = j % (3 * WIDTH // TN_PROJ)
    is_q = part < WIDTH // TN_PROJ
    is_v = part >= 2 * WIDTH // TN_PROJ
    acc = acc * jnp.where(is_q, QK_SCALE, 1.0)
    rows = i * TM_PROJ + lax.broadcasted_iota(jnp.int32, acc.shape, 0)
    acc = jnp.where(jnp.logical_and(is_v, rows < N_PAD), 0.0, acc)
    o_ref[...] = acc.astype(BF16)


def _qkv_proj(hb, w_in, layer):
    l, d = hb.shape
    return pl.pallas_call(
        _qkv_kernel,
        grid=(QKV_COLS // TN_PROJ, l // TM_PROJ),
        in_specs=[
            pl.BlockSpec((TM_PROJ, d), lambda j, i: (i, 0)),
            pl.BlockSpec((None, d, TN_PROJ), lambda j, i: (layer, 0, j)),
        ],
        out_specs=pl.BlockSpec((TM_PROJ, TN_PROJ), lambda j, i: (i, j)),
        out_shape=jax.ShapeDtypeStruct((l, QKV_COLS), BF16),
        scratch_shapes=[pltpu.VMEM((d, TN_PROJ), BF16)],
        compiler_params=_cparams(("arbitrary", "arbitrary")),
        name="qkv_proj",
    )(hb, w_in)


def _gate_kernel(x_ref, w_ref, o_ref):
    acc = jnp.dot(x_ref[...], w_ref[...], preferred_element_type=F32)
    o_ref[...] = jax.nn.sigmoid(acc).astype(BF16)


def _gate_proj(hb, wg):
    l, d = hb.shape
    n = wg.shape[1]
    return pl.pallas_call(
        _gate_kernel,
        grid=(n // TN_PROJ, l // TM_PROJ),
        in_specs=[
            pl.BlockSpec((TM_PROJ, d), lambda j, i: (i, 0)),
            pl.BlockSpec((d, TN_PROJ), lambda j, i: (0, j)),
        ],
        out_specs=pl.BlockSpec((TM_PROJ, TN_PROJ), lambda j, i: (i, j)),
        out_shape=jax.ShapeDtypeStruct((l, n), BF16),
        compiler_params=_cparams(("arbitrary", "arbitrary")),
        name="gate_proj",
    )(hb, wg)


def _fgate_kernel(h_ref, w_ref, b_ref, o_ref, carry_ref):
    i = pl.program_id(0)

    @pl.when(i == 0)
    def _():
        carry_ref[...] = jnp.zeros_like(carry_ref)

    f = jnp.dot(h_ref[...], w_ref[...], precision=lax.Precision.HIGHEST,
                preferred_element_type=F32) + b_ref[...]
    lf = _log_sigmoid(f)
    rows = i * TM_LN + lax.broadcasted_iota(jnp.int32, lf.shape, 0)
    lf = jnp.where(rows >= N_PAD, lf, 0.0)
    r = lax.broadcasted_iota(jnp.int32, (TM_LN, TM_LN), 0)
    c = lax.broadcasted_iota(jnp.int32, (TM_LN, TM_LN), 1)
    tri = jnp.where(r >= c, 1.0, 0.0).astype(F32)
    cs = jnp.dot(tri, lf, precision=lax.Precision.HIGHEST, preferred_element_type=F32) + carry_ref[...]
    o_ref[...] = cs
    carry_ref[...] = cs[TM_LN - 1:TM_LN, :]


def _forget_cumsum(hf, wf_pad, bf_pad):
    l, d = hf.shape
    return pl.pallas_call(
        _fgate_kernel,
        grid=(l // TM_LN,),
        in_specs=[
            pl.BlockSpec((TM_LN, d), lambda i: (i, 0)),
            pl.BlockSpec((d, LANE), lambda i: (0, 0)),
            pl.BlockSpec((1, LANE), lambda i: (0, 0)),
        ],
        out_specs=pl.BlockSpec((TM_LN, LANE), lambda i: (i, 0)),
        out_shape=jax.ShapeDtypeStruct((l, LANE), F32),
        scratch_shapes=[pltpu.VMEM((1, LANE), F32)],
        compiler_params=_cparams(("arbitrary",)),
        name="forget_cumsum",
    )(hf, wf_pad, bf_pad)


def _sb_kernel(q_ref, k_ref, v_ref, o_ref):
    i = pl.program_id(1)
    q = q_ref[...]
    tq, tk = SB_TQ, SB_TK
    r = lax.broadcasted_iota(jnp.int32, (2 * tk, 2 * tk), 0)
    c = lax.broadcasted_iota(jnp.int32, (2 * tk, 2 * tk), 1)
    suffix = jnp.where(jnp.logical_or(c >= tk, (r % tk) > c), 1.0, 0.0).astype(BF16)
    row = lax.broadcasted_iota(jnp.int32, (tq, tk), 0)
    col = lax.broadcasted_iota(jnp.int32, (tq, tk), 1)
    strictly_causal = col < row

    def block(j, run, acc, diagonal):
        start = pl.multiple_of(j * tk, tk)
        kj = k_ref[pl.ds(start, tk), :]
        vj = v_ref[pl.ds(start, tk), :]
        z = lax.dot_general(q, kj, (((1,), (1,)), ((), ())), preferred_element_type=F32)
        log_beta = _log_sigmoid(z)
        log_keep = log_beta - z
        if diagonal:
            log_keep = jnp.where(strictly_causal, log_keep, 0.0)
        hi = log_keep.astype(BF16)
        lo = (log_keep - hi.astype(F32)).astype(BF16)
        sums = jnp.dot(jnp.concatenate([hi, lo], axis=1), suffix, preferred_element_type=F32)
        w = jnp.exp(log_beta + sums[:, :tk] + run)
        if diagonal:
            w = jnp.where(strictly_causal, w, 0.0)
        acc = acc + jnp.dot(w.astype(BF16), vj, preferred_element_type=F32)
        return run + sums[:, tk:], acc

    run, acc = block(i, jnp.zeros((tq, tk), F32), jnp.zeros((tq, HEAD_DIM), F32), True)

    def cond(s):
        j, run, _ = s
        return jnp.logical_and(j >= 0, jnp.max(run) > SB_EXIT)

    def body(s):
        j, run, acc = s
        run, acc = block(j, run, acc, False)
        return j - 1, run, acc

    _, _, acc = lax.while_loop(cond, body, (i - 1, run, acc))
    o_ref[...] = acc.astype(BF16)


def _sb_attention(qkv):
    l = qkv.shape[0]
    w_blocks = WIDTH // HEAD_DIM
    return pl.pallas_call(
        _sb_kernel,
        grid=(N_HEADS, l // SB_TQ),
        in_specs=[
            pl.BlockSpec((SB_TQ, HEAD_DIM), lambda h, i: (i, h)),
            pl.BlockSpec((l, HEAD_DIM), lambda h, i: (0, w_blocks + h)),
            pl.BlockSpec((l, HEAD_DIM), lambda h, i: (0, 2 * w_blocks + h)),
        ],
        out_specs=pl.BlockSpec((SB_TQ, HEAD_DIM), lambda h, i: (i, h)),
        out_shape=jax.ShapeDtypeStruct((l, WIDTH), BF16),
        compiler_params=_cparams(("arbitrary", "arbitrary")),
        name="sb_attention",
    )(qkv, qkv, qkv)


def _fox_kernel(q_ref, k_ref, v_ref, b_ref, o_ref):
    i = pl.program_id(1)
    q = q_ref[...]
    t = FOX_T
    row = lax.broadcasted_iota(jnp.int32, (t, t), 0)
    col = lax.broadcasted_iota(jnp.int32, (t, t), 1)
    causal = col <= row

    def block(j, m, den, acc, diagonal):
        start = pl.multiple_of(j * t, t)
        kj = k_ref[pl.ds(start, t), :]
        vj = v_ref[pl.ds(start, t), :]
        s = lax.dot_general(q, kj, (((1,), (1,)), ((), ())), preferred_element_type=F32)
        s = s + b_ref[j]
        if diagonal:
            s = jnp.where(causal, s, NEG)
        m_new = jnp.maximum(m, jnp.max(s, axis=1, keepdims=True))
        scale = jnp.exp(m - m_new)
        p = jnp.exp(s - m_new)
        den = scale * den + jnp.sum(p, axis=1, keepdims=True)
        acc = scale * acc + jnp.dot(p.astype(BF16), vj, preferred_element_type=F32)
        return m_new, den, acc

    m, den, acc = block(i, jnp.full((t, 1), NEG, F32), jnp.zeros((t, 1), F32),
                        jnp.zeros((t, HEAD_DIM), F32), True)

    def body(j, s):
        return block(j, *s, False)

    m, den, acc = lax.fori_loop(0, i, body, (m, den, acc))
    o_ref[...] = (acc / den).astype(BF16)


def _fox_attention(qkv, key_bias):
    l = qkv.shape[0]
    w_blocks = WIDTH // HEAD_DIM
    nkb = l // FOX_T
    return pl.pallas_call(
        _fox_kernel,
        grid=(N_HEADS, l // FOX_T),
        in_specs=[
            pl.BlockSpec((FOX_T, HEAD_DIM), lambda h, i: (i, 3 * w_blocks + h)),
            pl.BlockSpec((l, HEAD_DIM), lambda h, i: (0, 4 * w_blocks + h)),
            pl.BlockSpec((l, HEAD_DIM), lambda h, i: (0, 5 * w_blocks + h)),
            pl.BlockSpec((None, nkb, 1, FOX_T), lambda h, i: (h, 0, 0, 0)),
        ],
        out_specs=pl.BlockSpec((FOX_T, HEAD_DIM), lambda h, i: (i, h)),
        out_shape=jax.ShapeDtypeStruct((l, WIDTH), BF16),
        compiler_params=_cparams(("arbitrary", "arbitrary")),
        name="fox_attention",
    )(qkv, qkv, qkv, key_bias)


def _merge_kernel(ys_ref, yf_ref, gs_ref, gf_ref, ws_ref, wf_ref, o_ref, wsb_ref, wfb_ref):
    i = pl.program_id(1)

    @pl.when(i == 0)
    def _():
        wsb_ref[...] = ws_ref[...].astype(BF16)
        wfb_ref[...] = wf_ref[...].astype(BF16)

    a = jnp.dot(ys_ref[...], wsb_ref[...], preferred_element_type=F32)
    b = jnp.dot(yf_ref[...], wfb_ref[...], preferred_element_type=F32)
    o_ref[...] = (gs_ref[...].astype(F32) * a + gf_ref[...].astype(F32) * b).astype(BF16)


def _merge(y_sb, y_fx, gates, w_bsb, w_bfx, layer):
    l = y_sb.shape[0]
    d = D_MODEL
    nj = d // TN_PROJ
    act = pl.BlockSpec((TM_PROJ, WIDTH), lambda j, i: (i, 0))
    wspec = pl.BlockSpec((None, WIDTH, TN_PROJ), lambda j, i: (layer, 0, j))
    return pl.pallas_call(
        _merge_kernel,
        grid=(nj, l // TM_PROJ),
        in_specs=[
            act, act,
            pl.BlockSpec((TM_PROJ, TN_PROJ), lambda j, i: (i, j)),
            pl.BlockSpec((TM_PROJ, TN_PROJ), lambda j, i: (i, nj + j)),
            wspec, wspec,
        ],
        out_specs=pl.BlockSpec((TM_PROJ, TN_PROJ), lambda j, i: (i, j)),
        out_shape=jax.ShapeDtypeStruct((l, d), BF16),
        scratch_shapes=[pltpu.VMEM((WIDTH, TN_PROJ), BF16), pltpu.VMEM((WIDTH, TN_PROJ), BF16)],
        compiler_params=_cparams(("arbitrary", "arbitrary")),
        name="merge",
    )(y_sb, y_fx, gates, gates, w_bsb, w_bfx)


def _out_ln_kernel(x_ref, w_ref, h_ref, g_ref, b_ref, wr_ref, of_ref, ob_ref, aff_ref, acc_ref):
    k = pl.program_id(1)

    @pl.when(k == 0)
    def _():
        acc_ref[...] = jnp.zeros_like(acc_ref)

    acc_ref[...] += jnp.dot(x_ref[...], w_ref[...].astype(BF16), preferred_element_type=F32)

    @pl.when(k == pl.num_programs(1) - 1)
    def _():
        y = _layer_norm(ALPHA * h_ref[...] + acc_ref[...], g_ref[...], b_ref[...])
        of_ref[...] = y
        ob_ref[...] = y.astype(BF16)
        logits = jnp.dot(y, wr_ref[...], precision=lax.Precision.HIGHEST, preferred_element_type=F32)
        aff_ref[...] = jax.nn.sigmoid(logits)


def _out_ln(merged, w_out, hf, g, b, wr_pad, layer):
    l, d = hf.shape
    row = pl.BlockSpec((TM_LN, d), lambda i, k: (i, 0))
    vec = pl.BlockSpec((None, 1, d), lambda i, k: (layer, 0, 0))
    return pl.pallas_call(
        _out_ln_kernel,
        grid=(l // TM_LN, d // TK_OUT),
        in_specs=[
            pl.BlockSpec((TM_LN, TK_OUT), lambda i, k: (i, k)),
            pl.BlockSpec((None, TK_OUT, d), lambda i, k: (layer, k, 0)),
            row, vec, vec,
            pl.BlockSpec((d, LANE), lambda i, k: (0, 0)),
        ],
        out_specs=[row, row, pl.BlockSpec((TM_LN, LANE), lambda i, k: (i, 0))],
        out_shape=[jax.ShapeDtypeStruct((l, d), F32), jax.ShapeDtypeStruct((l, d), BF16),
                   jax.ShapeDtypeStruct((l, LANE), F32)],
        scratch_shapes=[pltpu.VMEM((TM_LN, d), F32)],
        compiler_params=_cparams(("arbitrary", "arbitrary")),
        name="out_ln",
    )(merged, w_out, hf, g.reshape(DEPTH, 1, d), b.reshape(DEPTH, 1, d), wr_pad)


def _expert_kernel(be_ref, tok_ref, nu_ref, h_hbm, wg_ref, wu_ref, wd_ref, o_ref,
                   xf_ref, xb_ref, acc_ref, sem):
    blk = pl.program_id(0)
    f = pl.program_id(1)
    used = blk < nu_ref[0]

    def row_copy(r, src_row):
        return pltpu.make_async_copy(h_hbm.at[pl.ds(src_row, 1)], xf_ref.at[pl.ds(r, 1)], sem)

    @pl.when(jnp.logical_and(used, f == 0))
    def _():
        base = blk * MOE_TM

        def issue(r, carry):
            row_copy(r, tok_ref[base + r]).start()
            return carry

        lax.fori_loop(0, MOE_TM, issue, 0)

        def wait(r, carry):
            row_copy(r, 0).wait()
            return carry

        lax.fori_loop(0, MOE_TM, wait, 0)
        xb_ref[...] = xf_ref[...].astype(BF16)
        acc_ref[...] = jnp.zeros_like(acc_ref)

    @pl.when(used)
    def _():
        x = xb_ref[...]
        g = jnp.dot(x, wg_ref[...].astype(BF16), preferred_element_type=F32)
        u = jnp.dot(x, wu_ref[...].astype(BF16), preferred_element_type=F32)
        a = (g * jax.nn.sigmoid(g) * u).astype(BF16)
        acc_ref[...] += jnp.dot(a, wd_ref[...].astype(BF16), preferred_element_type=F32)

    @pl.when(f == pl.num_programs(1) - 1)
    def _():
        o_ref[...] = jnp.where(used, acc_ref[...], 0.0)


def _expert_ffn(block_e, buf_tok, n_used, hf, w_gate, w_up, w_down, layer):
    l, d = hf.shape
    p = buf_tok.shape[0]
    nb = p // MOE_TM
    nf = EXPERT_FF // MOE_TF

    def w_in_map(b, f, be, tok, nu):
        live = b < nu[0]
        return (layer, be[b], 0, jnp.where(live, f, nf - 1))

    def w_out_map(b, f, be, tok, nu):
        live = b < nu[0]
        return (layer, be[b], jnp.where(live, f, nf - 1), 0)

    grid_spec = pltpu.PrefetchScalarGridSpec(
        num_scalar_prefetch=3,
        grid=(nb, nf),
        in_specs=[
            pl.BlockSpec(memory_space=pl.ANY),
            pl.BlockSpec((None, None, d, MOE_TF), w_in_map),
            pl.BlockSpec((None, None, d, MOE_TF), w_in_map),
            pl.BlockSpec((None, None, MOE_TF, d), w_out_map),
        ],
        out_specs=pl.BlockSpec((MOE_TM, d), lambda b, f, be, tok, nu: (b, 0)),
        scratch_shapes=[
            pltpu.VMEM((MOE_TM, d), F32),
            pltpu.VMEM((MOE_TM, d), BF16),
            pltpu.VMEM((MOE_TM, d), F32),
            pltpu.SemaphoreType.DMA(()),
        ],
    )
    return pl.pallas_call(
        _expert_kernel,
        grid_spec=grid_spec,
        out_shape=jax.ShapeDtypeStruct((p, d), F32),
        compiler_params=_cparams(("arbitrary", "arbitrary")),
        name="expert_ffn",
    )(block_e, buf_tok, n_used, hf, w_gate, w_up, w_down)


def _combine_kernel(pos_ref, yb_hbm, h_ref, w_ref, g_ref, b_ref, of_ref, ob_ref, y0_ref, y1_ref, sem):
    i = pl.program_id(0)
    base = i * CMB_TM * TOP_K

    def row_copy(dst, r, src_row):
        return pltpu.make_async_copy(yb_hbm.at[pl.ds(src_row, 1)], dst.at[pl.ds(r, 1)], sem)

    def issue(r, carry):
        row_copy(y0_ref, r, pos_ref[base + TOP_K * r]).start()
        row_copy(y1_ref, r, pos_ref[base + TOP_K * r + 1]).start()
        return carry

    lax.fori_loop(0, CMB_TM, issue, 0)

    def wait(r, carry):
        row_copy(y0_ref, r, 0).wait()
        row_copy(y1_ref, r, 0).wait()
        return carry

    lax.fori_loop(0, CMB_TM, wait, 0)
    w = w_ref[...]
    y = w[:, 0:1] * y0_ref[...] + w[:, 1:2] * y1_ref[...]
    out = _layer_norm(ALPHA * h_ref[...] + y, g_ref[...], b_ref[...])
    of_ref[...] = out
    ob_ref[...] = out.astype(BF16)


def _combine_ln(pos, yb, hf, wts, g, b, layer):
    l, d = hf.shape
    row = lambda i, pos: (i, 0)
    grid_spec = pltpu.PrefetchScalarGridSpec(
        num_scalar_prefetch=1,
        grid=(l // CMB_TM,),
        in_specs=[
            pl.BlockSpec(memory_space=pl.ANY),
            pl.BlockSpec((CMB_TM, d), row),
            pl.BlockSpec((CMB_TM, TOP_K), row),
            pl.BlockSpec((None, 1, d), lambda i, pos: (layer, 0, 0)),
            pl.BlockSpec((None, 1, d), lambda i, pos: (layer, 0, 0)),
        ],
        out_specs=[pl.BlockSpec((CMB_TM, d), row), pl.BlockSpec((CMB_TM, d), row)],
        scratch_shapes=[
            pltpu.VMEM((CMB_TM, d), F32),
            pltpu.VMEM((CMB_TM, d), F32),
            pltpu.SemaphoreType.DMA(()),
        ],
    )
    return pl.pallas_call(
        _combine_kernel,
        grid_spec=grid_spec,
        out_shape=[jax.ShapeDtypeStruct((l, d), F32), jax.ShapeDtypeStruct((l, d), BF16)],
        compiler_params=_cparams(("arbitrary",)),
        name="combine_ln",
    )(pos, yb, hf, wts, g.reshape(DEPTH, 1, d), b.reshape(DEPTH, 1, d))


def _route(aff, router_bias):
    n = aff.shape[0]
    sel = aff + router_bias.astype(F32)
    grp = lax.top_k(sel.reshape(n, N_GROUPS, EXPERTS_PER_GROUP), TOP_K)[0].sum(-1)
    best = jnp.argmax(grp, axis=-1)
    in_group = (jnp.arange(N_EXPERTS) // EXPERTS_PER_GROUP)[None, :] == best[:, None]
    _, idx = lax.top_k(jnp.where(in_group, sel, -jnp.inf), TOP_K)
    w = jnp.take_along_axis(aff, idx, axis=-1)
    return idx.astype(jnp.int32), w / jnp.sum(w, axis=-1, keepdims=True)


def _plan(idx):
    n = idx.shape[0]
    n_slots = n * TOP_K
    nb = -(-(n_slots + N_EXPERTS * (MOE_TM - 1)) // MOE_TM)
    slot_e = idx.reshape(-1)
    onehot = (slot_e[:, None] == jnp.arange(N_EXPERTS, dtype=jnp.int32)[None, :]).astype(jnp.int32)
    csum = jnp.cumsum(onehot, axis=0)
    rank = jnp.take_along_axis(csum, slot_e[:, None], axis=1)[:, 0] - 1
    counts = csum[-1]
    blocks_e = (counts + MOE_TM - 1) // MOE_TM
    bend = jnp.cumsum(blocks_e)
    bstart = bend - blocks_e
    pos = (bstart[slot_e] * MOE_TM + rank).astype(jnp.int32)
    n_used = bend[-1:].astype(jnp.int32)
    block_e = jnp.minimum(jnp.searchsorted(bend, jnp.arange(nb), side='right'),
                          N_EXPERTS - 1).astype(jnp.int32)
    slot_tok = jnp.arange(n_slots, dtype=jnp.int32) // TOP_K
    buf_tok = jnp.zeros((nb * MOE_TM,), jnp.int32).at[pos].set(slot_tok)
    return pos, block_e, buf_tok, n_used


def kernel(x, meta_tokens, ln_in_g, ln_in_b, w_in, b_forget, w_branch_sb, w_branch_fox, w_out,
           ln_mix_g, ln_mix_b, w_router, router_bias, w_gate, w_up, w_down, ln_ffn_g, ln_ffn_b):
    b, s, d = x.shape
    assert b == 1 and d == D_MODEL
    l = s + PREFIX
    h0 = jnp.concatenate([jnp.zeros((N_PAD, d), x.dtype), meta_tokens.astype(x.dtype), x[0]], axis=0)
    hf, hb = _ln_in(h0, ln_in_g, ln_in_b)

    wr_pad = jnp.pad(w_router.astype(F32), ((0, 0), (0, LANE - N_EXPERTS)))
    key_is_pad = (jnp.arange(l) < N_PAD)[None, :]
    for layer in range(DEPTH):
        w_gates = w_in[layer, :, G_COL0:].astype(BF16)
        wf_pad = jnp.pad(w_in[layer, :, F_COL0:G_COL0], ((0, 0), (0, LANE - N_HEADS)))
        bf_pad = jnp.pad(b_forget[layer], (0, LANE - N_HEADS)).reshape(1, LANE)

        qkv = _qkv_proj(hb, w_in, layer)
        gates = _gate_proj(hb, w_gates)
        log_f_cum = _forget_cumsum(hf, wf_pad, bf_pad)[:, :N_HEADS]
        key_bias = jnp.where(key_is_pad, NEG, -log_f_cum.T).reshape(N_HEADS, l // FOX_T, 1, FOX_T)
        y_sb = _sb_attention(qkv)
        y_fx = _fox_attention(qkv, key_bias)
        merged = _merge(y_sb, y_fx, gates, w_branch_sb, w_branch_fox, layer)
        hf, hb, aff = _out_ln(merged, w_out, hf, ln_mix_g, ln_mix_b, wr_pad, layer)

        idx, wts = _route(aff[:, :N_EXPERTS], router_bias)
        pos, block_e, buf_tok, n_used = _plan(idx)
        yb = _expert_ffn(block_e, buf_tok, n_used, hf, w_gate, w_up, w_down, layer)
        hf, hb = _combine_ln(pos, yb, hf, wts.astype(F32), ln_ffn_g, ln_ffn_b, layer)
    return hf[PREFIX:][None]
```

```python
import functools

import jax
import jax.numpy as jnp
from jax import lax
from jax.experimental import pallas as pl
from jax.experimental.pallas import tpu as pltpu

F32 = jnp.float32
BF16 = jnp.bfloat16

D_MODEL = 2048
DEPTH = 2
HEAD_DIM = 128
N_HEADS = 8
WIDTH = N_HEADS * HEAD_DIM
N_META = 16
N_PAD = 112
PREFIX = N_PAD + N_META
N_EXPERTS = 16
N_GROUPS = 4
EXPERTS_PER_GROUP = N_EXPERTS // N_GROUPS
TOP_K = 2
EXPERT_FF = 1024
ALPHA = (2 * DEPTH) ** 0.25
LN_EPS = 1e-5
NEG = -1e30
QK_SCALE = HEAD_DIM ** -0.5
QKV_COLS = 6 * WIDTH
F_COL0 = QKV_COLS
G_COL0 = QKV_COLS + N_HEADS

SB_EXIT = -104.0

LANE = 128
VMEM_LIMIT = 56 * 1024 * 1024
TM_PROJ = 1664
TN_PROJ = 512
TM_LN = 640
TK_OUT = 512
SB_TQ = 128
SB_TK = 128
FOX_T = 640
MOE_TM = 512
MOE_TF = 256
CMB_TM = 128
DISPATCH_CHUNK = 128


def _cparams(sem):
    return pltpu.CompilerParams(dimension_semantics=sem, vmem_limit_bytes=VMEM_LIMIT)


def _layer_norm(x, g, b):
    mu = jnp.mean(x, axis=-1, keepdims=True)
    xc = x - mu
    var = jnp.mean(xc * xc, axis=-1, keepdims=True)
    return xc * lax.rsqrt(var + LN_EPS) * g + b


def _log_sigmoid(z):
    return jnp.minimum(z, 0.0) - jnp.log(1.0 + jnp.exp(-jnp.abs(z)))


def _ln_in_kernel(p_ref, x_ref, g_ref, b_ref, of_ref, ob_ref):
    src = jnp.where(pl.program_id(0) == 0, p_ref[...], x_ref[...])
    y = _layer_norm(src, g_ref[...], b_ref[...])
    of_ref[...] = y
    ob_ref[...] = y.astype(BF16)


def _ln_in(prefix, x, g, b):
    s, d = x.shape
    l = s + PREFIX
    row = pl.BlockSpec((PREFIX, d), lambda i: (i, 0))
    vec = pl.BlockSpec((1, d), lambda i: (0, 0))
    return pl.pallas_call(
        _ln_in_kernel,
        grid=(l // PREFIX,),
        in_specs=[
            pl.BlockSpec((PREFIX, d), lambda i: (0, 0)),
            pl.BlockSpec((PREFIX, d), lambda i: (jnp.maximum(i - 1, 0), 0)),
            vec, vec,
        ],
        out_specs=[row, row],
        out_shape=[jax.ShapeDtypeStruct((l, d), F32), jax.ShapeDtypeStruct((l, d), BF16)],
        compiler_params=_cparams(("arbitrary",)),
        name="ln_in",
    )(prefix, x, g.reshape(1, d), b.reshape(1, d))


def _qkv_kernel(x_ref, w_ref, o_ref, wb_ref):
    j = pl.program_id(0)
    i = pl.program_id(1)

    @pl.when(i == 0)
    def _():
        wb_ref[...] = w_ref[...].astype(BF16)

    acc = jnp.dot(x_ref[...], wb_ref[...], preferred_element_type=F32)
    part = j % (3 * WIDTH // TN_PROJ)
    is_q = part < WIDTH // TN_PROJ
    is_v = part >= 2 * WIDTH // TN_PROJ
    acc = acc * jnp.where(is_q, QK_SCALE, 1.0)
    rows = i * TM_PROJ + lax.broadcasted_iota(jnp.int32, acc.shape, 0)
    acc = jnp.where(jnp.logical_and(is_v, rows < N_PAD), 0.0, acc)
    o_ref[...] = acc.astype(BF16)


def _qkv_proj(hb, w_in, layer):
    l, d = hb.shape
    return pl.pallas_call(
        _qkv_kernel,
        grid=(QKV_COLS // TN_PROJ, l // TM_PROJ),
        in_specs=[
            pl.BlockSpec((TM_PROJ, d), lambda j, i: (i, 0)),
            pl.BlockSpec((None, d, TN_PROJ), lambda j, i: (layer, 0, j)),
        ],
        out_specs=pl.BlockSpec((TM_PROJ, TN_PROJ), lambda j, i: (i, j)),
        out_shape=jax.ShapeDtypeStruct((l, QKV_COLS), BF16),
        scratch_shapes=[pltpu.VMEM((d, TN_PROJ), BF16)],
        compiler_params=_cparams(("arbitrary", "arbitrary")),
        name="qkv_proj",
    )(hb, w_in)


def _gate_kernel(x_ref, w_ref, o_ref):
    acc = jnp.dot(x_ref[...], w_ref[...], preferred_element_type=F32)
    o_ref[...] = jax.nn.sigmoid(acc).astype(BF16)


def _gate_proj(hb, wg):
    l, d = hb.shape
    n = wg.shape[1]
    return pl.pallas_call(
        _gate_kernel,
        grid=(n // TN_PROJ, l // TM_PROJ),
        in_specs=[
            pl.BlockSpec((TM_PROJ, d), lambda j, i: (i, 0)),
            pl.BlockSpec((d, TN_PROJ), lambda j, i: (0, j)),
        ],
        out_specs=pl.BlockSpec((TM_PROJ, TN_PROJ), lambda j, i: (i, j)),
        out_shape=jax.ShapeDtypeStruct((l, n), BF16),
        compiler_params=_cparams(("arbitrary", "arbitrary")),
        name="gate_proj",
    )(hb, wg)


def _fgate_kernel(h_ref, w_ref, b_ref, o_ref, carry_ref):
    i = pl.program_id(0)

    @pl.when(i == 0)
    def _():
        carry_ref[...] = jnp.zeros_like(carry_ref)

    f = jnp.dot(h_ref[...], w_ref[...], precision=lax.Precision.HIGHEST,
                preferred_element_type=F32) + b_ref[...]
    lf = _log_sigmoid(f)
    rows = i * TM_LN + lax.broadcasted_iota(jnp.int32, lf.shape, 0)
    lf = jnp.where(rows >= N_PAD, lf, 0.0)
    r = lax.broadcasted_iota(jnp.int32, (TM_LN, TM_LN), 0)
    c = lax.broadcasted_iota(jnp.int32, (TM_LN, TM_LN), 1)
    tri = jnp.where(r >= c, 1.0, 0.0).astype(F32)
    cs = jnp.dot(tri, lf, precision=lax.Precision.HIGHEST, preferred_element_type=F32) + carry_ref[...]
    o_ref[...] = cs
    carry_ref[...] = cs[TM_LN - 1:TM_LN, :]


def _forget_cumsum(hf, wf_pad, bf_pad):
    l, d = hf.shape
    return pl.pallas_call(
        _fgate_kernel,
        grid=(l // TM_LN,),
        in_specs=[
            pl.BlockSpec((TM_LN, d), lambda i: (i, 0)),
            pl.BlockSpec((d, LANE), lambda i: (0, 0)),
            pl.BlockSpec((1, LANE), lambda i: (0, 0)),
        ],
        out_specs=pl.BlockSpec((TM_LN, LANE), lambda i: (i, 0)),
        out_shape=jax.ShapeDtypeStruct((l, LANE), F32),
        scratch_shapes=[pltpu.VMEM((1, LANE), F32)],
        compiler_params=_cparams(("arbitrary",)),
        name="forget_cumsum",
    )(hf, wf_pad, bf_pad)


def _sb_kernel(q_ref, k_ref, v_ref, o_ref, run_ref, acc_ref):
    i = pl.program_id(0)
    tq, tk = SB_TQ, SB_TK
    r = lax.broadcasted_iota(jnp.int32, (2 * tk, 2 * tk), 0)
    c = lax.broadcasted_iota(jnp.int32, (2 * tk, 2 * tk), 1)
    suffix = jnp.where(jnp.logical_or(c >= tk, (r % tk) > c), 1.0, 0.0).astype(BF16)
    row = lax.broadcasted_iota(jnp.int32, (tq, tk), 0)
    col = lax.broadcasted_iota(jnp.int32, (tq, tk), 1)
    strictly_causal = col < row

    def block(j, diagonal):
        start = pl.multiple_of(j * tk, tk)
        worst = None
        for hd in range(N_HEADS):
            cols = slice(hd * HEAD_DIM, (hd + 1) * HEAD_DIM)
            q = q_ref[:, cols]
            kj = k_ref[pl.ds(start, tk), cols]
            vj = v_ref[pl.ds(start, tk), cols]
            z = lax.dot_general(q, kj, (((1,), (1,)), ((), ())), preferred_element_type=F32)
            log_beta = _log_sigmoid(z)
            log_keep = log_beta - z
            if diagonal:
                log_keep = jnp.where(strictly_causal, log_keep, 0.0)
            hi = log_keep.astype(BF16)
            lo = (log_keep - hi.astype(F32)).astype(BF16)
            sums = jnp.dot(jnp.concatenate([hi, lo], axis=1), suffix, preferred_element_type=F32)
            if diagonal:
                w = jnp.where(strictly_causal, jnp.exp(log_beta + sums[:, :tk]), 0.0)
                run = sums[:, tk:]
                acc_ref[:, cols] = jnp.dot(w.astype(BF16), vj, preferred_element_type=F32)
            else:
                run = run_ref[:, cols]
                w = jnp.exp(log_beta + sums[:, :tk] + run)
                run = run + sums[:, tk:]
                acc_ref[:, cols] += jnp.dot(w.astype(BF16), vj, preferred_element_type=F32)
            run_ref[:, cols] = run
            worst = run if worst is None else jnp.maximum(worst, run)
        return jnp.max(worst)

    worst = block(i, True)

    def cond(s):
        j, worst = s
        return jnp.logical_and(j >= 0, worst > SB_EXIT)

    def body(s):
        j, _ = s
        return j - 1, block(j, False)

    lax.while_loop(cond, body, (i - 1, worst))
    o_ref[...] = acc_ref[...].astype(BF16)


def _sb_attention(qkv):
    l = qkv.shape[0]
    resident = lambda blk: pl.BlockSpec((l, WIDTH), lambda i: (0, blk), pipeline_mode=pl.Buffered(1))
    return pl.pallas_call(
        _sb_kernel,
        grid=(l // SB_TQ,),
        in_specs=[pl.BlockSpec((SB_TQ, WIDTH), lambda i: (i, 0)), resident(1), resident(2)],
        out_specs=pl.BlockSpec((SB_TQ, WIDTH), lambda i: (i, 0)),
        out_shape=jax.ShapeDtypeStruct((l, WIDTH), BF16),
        scratch_shapes=[pltpu.VMEM((SB_TQ, WIDTH), F32), pltpu.VMEM((SB_TQ, WIDTH), F32)],
        compiler_params=_cparams(("arbitrary",)),
        name="sb_attention",
    )(qkv, qkv, qkv)


def _fox_kernel(q_ref, k_ref, v_ref, b_ref, o_ref):
    i = pl.program_id(1)
    q = q_ref[...]
    t = FOX_T
    row = lax.broadcasted_iota(jnp.int32, (t, t), 0)
    col = lax.broadcasted_iota(jnp.int32, (t, t), 1)
    causal = col <= row

    def block(j, m, den, acc, diagonal):
        start = pl.multiple_of(j * t, t)
        kj = k_ref[pl.ds(start, t), :]
        vj = v_ref[pl.ds(start, t), :]
        s = lax.dot_general(q, kj, (((1,), (1,)), ((), ())), preferred_element_type=F32)
        s = s + b_ref[j]
        if diagonal:
            s = jnp.where(causal, s, NEG)
        m_new = jnp.maximum(m, jnp.max(s, axis=1, keepdims=True))
        scale = jnp.exp(m - m_new)
        p = jnp.exp(s - m_new)
        den = scale * den + jnp.sum(p, axis=1, keepdims=True)
        acc = scale * acc + jnp.dot(p.astype(BF16), vj, preferred_element_type=F32)
        return m_new, den, acc

    m, den, acc = block(i, jnp.full((t, 1), NEG, F32), jnp.zeros((t, 1), F32),
                        jnp.zeros((t, HEAD_DIM), F32), True)

    def body(j, s):
        return block(j, *s, False)

    m, den, acc = lax.fori_loop(0, i, body, (m, den, acc))
    o_ref[...] = (acc / den).astype(BF16)


def _fox_attention(qkv, key_bias):
    l = qkv.shape[0]
    w_blocks = WIDTH // HEAD_DIM
    nkb = l // FOX_T
    return pl.pallas_call(
        _fox_kernel,
        grid=(N_HEADS, l // FOX_T),
        in_specs=[
            pl.BlockSpec((FOX_T, HEAD_DIM), lambda h, i: (i, 3 * w_blocks + h)),
            pl.BlockSpec((l, HEAD_DIM), lambda h, i: (0, 4 * w_blocks + h)),
            pl.BlockSpec((l, HEAD_DIM), lambda h, i: (0, 5 * w_blocks + h)),
            pl.BlockSpec((None, nkb, 1, FOX_T), lambda h, i: (h, 0, 0, 0)),
        ],
        out_specs=pl.BlockSpec((FOX_T, HEAD_DIM), lambda h, i: (i, h)),
        out_shape=jax.ShapeDtypeStruct((l, WIDTH), BF16),
        compiler_params=_cparams(("arbitrary", "arbitrary")),
        name="fox_attention",
    )(qkv, qkv, qkv, key_bias)


def _merge_kernel(ys_ref, yf_ref, gs_ref, gf_ref, ws_ref, wf_ref, o_ref, wsb_ref, wfb_ref):
    i = pl.program_id(1)

    @pl.when(i == 0)
    def _():
        wsb_ref[...] = ws_ref[...].astype(BF16)
        wfb_ref[...] = wf_ref[...].astype(BF16)

    a = jnp.dot(ys_ref[...], wsb_ref[...], preferred_element_type=F32)
    b = jnp.dot(yf_ref[...], wfb_ref[...], preferred_element_type=F32)
    o_ref[...] = (gs_ref[...].astype(F32) * a + gf_ref[...].astype(F32) * b).astype(BF16)


def _merge(y_sb, y_fx, gates, w_bsb, w_bfx, layer):
    l = y_sb.shape[0]
    d = D_MODEL
    nj = d // TN_PROJ
    act = pl.BlockSpec((TM_PROJ, WIDTH), lambda j, i: (i, 0))
    wspec = pl.BlockSpec((None, WIDTH, TN_PROJ), lambda j, i: (layer, 0, j))
    return pl.pallas_call(
        _merge_kernel,
        grid=(nj, l // TM_PROJ),
        in_specs=[
            act, act,
            pl.BlockSpec((TM_PROJ, TN_PROJ), lambda j, i: (i, j)),
            pl.BlockSpec((TM_PROJ, TN_PROJ), lambda j, i: (i, nj + j)),
            wspec, wspec,
        ],
        out_specs=pl.BlockSpec((TM_PROJ, TN_PROJ), lambda j, i: (i, j)),
        out_shape=jax.ShapeDtypeStruct((l, d), BF16),
        scratch_shapes=[pltpu.VMEM((WIDTH, TN_PROJ), BF16), pltpu.VMEM((WIDTH, TN_PROJ), BF16)],
        compiler_params=_cparams(("arbitrary", "arbitrary")),
        name="merge",
    )(y_sb, y_fx, gates, gates, w_bsb, w_bfx)


def _pick(index, values):
    out = values[0]
    for k in range(1, len(values)):
        out = jnp.where(index == k, values[k], out)
    return out


def _grouped_top2(sel, aff):
    srow = [sel[e:e + 1, :] for e in range(N_EXPERTS)]
    arow = [aff[e:e + 1, :] for e in range(N_EXPERTS)]
    n = EXPERTS_PER_GROUP
    score = []
    for g in range(N_GROUPS):
        v = srow[g * n:(g + 1) * n]
        pair = [v[a] + v[b] for a in range(n) for b in range(a + 1, n)]
        score.append(functools.reduce(jnp.maximum, pair))
    group = jnp.zeros_like(score[0], dtype=jnp.int32)
    best = score[0]
    for g in range(1, N_GROUPS):
        better = score[g] > best
        group = jnp.where(better, g, group)
        best = jnp.where(better, score[g], best)
    v = [_pick(group, [srow[g * n + k] for g in range(N_GROUPS)]) for k in range(n)]
    a = [_pick(group, [arow[g * n + k] for g in range(N_GROUPS)]) for k in range(n)]
    i1 = jnp.zeros_like(group)
    m1 = v[0]
    for k in range(1, n):
        better = v[k] > m1
        i1 = jnp.where(better, k, i1)
        m1 = jnp.where(better, v[k], m1)
    first_is_0 = i1 == 0
    i2 = jnp.where(first_is_0, 1, 0)
    m2 = jnp.where(first_is_0, v[1], v[0])
    for k in range(1, n):
        better = jnp.logical_and(i1 != k, v[k] > m2)
        i2 = jnp.where(better, k, i2)
        m2 = jnp.where(better, v[k], m2)
    a1 = _pick(i1, a)
    a2 = _pick(i2, a)
    tot = a1 + a2
    return group * n + i1, group * n + i2, a1 / tot, a2 / tot


def _out_ln_kernel(x_ref, w_ref, h_ref, g_ref, b_ref, wr_ref, rb_ref,
                   of_ref, ob_ref, hp_ref, ri_ref, rw_ref, cnt_ref, acc_ref):
    i = pl.program_id(0)
    k = pl.program_id(1)

    @pl.when(k == 0)
    def _():
        acc_ref[...] = jnp.zeros_like(acc_ref)

    @pl.when(jnp.logical_and(i == 0, k == 0))
    def _():
        cnt_ref[...] = jnp.zeros_like(cnt_ref)

    acc_ref[...] += jnp.dot(x_ref[...], w_ref[...].astype(BF16), preferred_element_type=F32)

    @pl.when(k == pl.num_programs(1) - 1)
    def _():
        y = _layer_norm(ALPHA * h_ref[...] + acc_ref[...], g_ref[...], b_ref[...])
        of_ref[...] = y
        yb = y.astype(BF16)
        ob_ref[...] = yb
        bits = pltpu.bitcast(yb.astype(F32), jnp.uint32)
        half = D_MODEL // 2
        hp_ref[...] = (bits[:, :half] >> 16) | (bits[:, half:] & jnp.uint32(0xFFFF0000))

        logits = lax.dot_general(wr_ref[...], y, (((1,), (1,)), ((), ())),
                                 precision=lax.Precision.HIGHEST, preferred_element_type=F32)
        aff = jax.nn.sigmoid(logits)
        e1, e2, w1, w2 = _grouped_top2(aff + rb_ref[...], aff)
        t = TM_LN
        eid = lax.broadcasted_iota(jnp.int32, (N_EXPERTS, t), 0)
        hit1 = eid == e1
        hit2 = eid == e2
        onehot = jnp.where(jnp.logical_or(hit1, hit2), 1.0, 0.0)
        r = lax.broadcasted_iota(jnp.int32, (t, t), 0)
        c = lax.broadcasted_iota(jnp.int32, (t, t), 1)
        before = jnp.where(r < c, 1.0, 0.0).astype(BF16)
        seen = jnp.dot(onehot.astype(BF16), before, preferred_element_type=F32) + cnt_ref[:, 0:1]
        rank1 = jnp.sum(jnp.where(hit1, seen, 0.0), axis=0, keepdims=True)
        rank2 = jnp.sum(jnp.where(hit2, seen, 0.0), axis=0, keepdims=True)
        ri_ref[0:1, :] = e1
        ri_ref[1:2, :] = e2
        ri_ref[2:3, :] = rank1.astype(jnp.int32)
        ri_ref[3:4, :] = rank2.astype(jnp.int32)
        ri_ref[4:8, :] = jnp.zeros((4, t), jnp.int32)
        rw_ref[0:1, :] = w1
        rw_ref[1:2, :] = w2
        rw_ref[2:8, :] = jnp.zeros((6, t), F32)
        cnt_ref[...] += jnp.sum(onehot, axis=1, keepdims=True)


def _out_ln(merged, w_out, hf, g, b, wr_t, rb, layer):
    l, d = hf.shape
    row = pl.BlockSpec((TM_LN, d), lambda i, k: (i, 0))
    vec = pl.BlockSpec((None, 1, d), lambda i, k: (layer, 0, 0))
    tok = pl.BlockSpec((8, TM_LN), lambda i, k: (0, i))
    return pl.pallas_call(
        _out_ln_kernel,
        grid=(l // TM_LN, d // TK_OUT),
        in_specs=[
            pl.BlockSpec((TM_LN, TK_OUT), lambda i, k: (i, k)),
            pl.BlockSpec((None, TK_OUT, d), lambda i, k: (layer, k, 0)),
            row, vec, vec,
            pl.BlockSpec((N_EXPERTS, d), lambda i, k: (0, 0)),
            pl.BlockSpec((N_EXPERTS, TM_LN), lambda i, k: (0, 0)),
        ],
        out_specs=[row, row, pl.BlockSpec((TM_LN, d // 2), lambda i, k: (i, 0)), tok, tok,
                   pl.BlockSpec((N_EXPERTS, LANE), lambda i, k: (0, 0))],
        out_shape=[jax.ShapeDtypeStruct((l, d), F32), jax.ShapeDtypeStruct((l, d), BF16),
                   jax.ShapeDtypeStruct((l, d // 2), jnp.uint32),
                   jax.ShapeDtypeStruct((8, l), jnp.int32), jax.ShapeDtypeStruct((8, l), F32),
                   jax.ShapeDtypeStruct((N_EXPERTS, LANE), F32)],
        scratch_shapes=[pltpu.VMEM((TM_LN, d), F32)],
        compiler_params=_cparams(("arbitrary", "arbitrary")),
        name="out_ln",
    )(merged, w_out, hf, g.reshape(DEPTH, 1, d), b.reshape(DEPTH, 1, d), wr_t, rb)


def _dispatch_kernel(pos1_ref, pos2_ref, zs_ref, hp_hbm, xs_hbm, zero_ref, sem):
    l = hp_hbm.shape[0]
    zero_ref[...] = jnp.zeros_like(zero_ref)

    def fill(b):
        return pltpu.make_async_copy(zero_ref, xs_hbm.at[pl.ds(b * MOE_TM, MOE_TM)], sem)

    n_blocks = xs_hbm.shape[0] // MOE_TM
    for b in range(n_blocks):
        @pl.when(zs_ref[b] != 0)
        def _():
            fill(b).start()
    for b in range(n_blocks):
        @pl.when(zs_ref[b] != 0)
        def _():
            fill(b).wait()

    def row_copy(t, dst):
        return pltpu.make_async_copy(hp_hbm.at[pl.ds(t, 1)], xs_hbm.at[pl.ds(dst, 1)], sem)

    def wait(t, carry):
        row_copy(0, 0).wait()
        row_copy(0, 0).wait()
        return carry

    def chunk(c, carry):
        def issue(r, carry):
            t = c * DISPATCH_CHUNK + r
            row_copy(t, pos1_ref[t]).start()
            row_copy(t, pos2_ref[t]).start()
            return carry

        lax.fori_loop(0, DISPATCH_CHUNK, issue, 0, unroll=8)

        @pl.when(c > 0)
        def _():
            lax.fori_loop(0, DISPATCH_CHUNK, wait, 0, unroll=8)

        return carry

    lax.fori_loop(0, l // DISPATCH_CHUNK, chunk, 0)
    lax.fori_loop(0, DISPATCH_CHUNK, wait, 0, unroll=8)


def _dispatch(pos1, pos2, zstart, hp, n_rows):
    l, half = hp.shape
    grid_spec = pltpu.PrefetchScalarGridSpec(
        num_scalar_prefetch=3,
        grid=(1,),
        in_specs=[pl.BlockSpec(memory_space=pl.ANY)],
        out_specs=pl.BlockSpec(memory_space=pl.ANY),
        scratch_shapes=[pltpu.VMEM((MOE_TM, half), jnp.uint32), pltpu.SemaphoreType.DMA(())],
    )
    return pl.pallas_call(
        _dispatch_kernel,
        grid_spec=grid_spec,
        out_shape=jax.ShapeDtypeStruct((n_rows, half), jnp.uint32),
        compiler_params=_cparams(("arbitrary",)),
        name="dispatch",
    )(pos1, pos2, zstart, hp)


def _expert_kernel(be_ref, nu_ref, xs_ref, wg_ref, wu_ref, wd_ref, o_ref, xb_ref, acc_ref):
    blk = pl.program_id(0)
    f = pl.program_id(1)
    used = blk < nu_ref[0]

    @pl.when(jnp.logical_and(used, f == 0))
    def _():
        words = xs_ref[...]
        lo = pltpu.bitcast(words << 16, F32)
        hi = pltpu.bitcast(words & jnp.uint32(0xFFFF0000), F32)
        half = D_MODEL // 2
        xb_ref[:, :half] = lo.astype(BF16)
        xb_ref[:, half:] = hi.astype(BF16)
        acc_ref[...] = jnp.zeros_like(acc_ref)

    @pl.when(used)
    def _():
        x = xb_ref[...]
        g = jnp.dot(x, wg_ref[...].astype(BF16), preferred_element_type=F32)
        u = jnp.dot(x, wu_ref[...].astype(BF16), preferred_element_type=F32)
        a = (g * jax.nn.sigmoid(g) * u).astype(BF16)
        acc_ref[...] += jnp.dot(a, wd_ref[...].astype(BF16), preferred_element_type=F32)

    @pl.when(f == pl.num_programs(1) - 1)
    def _():
        o_ref[...] = jnp.where(used, acc_ref[...], 0.0)


def _expert_ffn(block_e, n_used, xs, w_gate, w_up, w_down, layer):
    d = D_MODEL
    nb = block_e.shape[0]
    nf = EXPERT_FF // MOE_TF

    def x_map(b, f, be, nu):
        return (jnp.minimum(b, nu[0] - 1), 0)

    def w_in_map(b, f, be, nu):
        return (layer, be[b], 0, jnp.where(b < nu[0], f, nf - 1))

    def w_out_map(b, f, be, nu):
        return (layer, be[b], jnp.where(b < nu[0], f, nf - 1), 0)

    grid_spec = pltpu.PrefetchScalarGridSpec(
        num_scalar_prefetch=2,
        grid=(nb, nf),
        in_specs=[
            pl.BlockSpec((MOE_TM, d // 2), x_map),
            pl.BlockSpec((None, None, d, MOE_TF), w_in_map),
            pl.BlockSpec((None, None, d, MOE_TF), w_in_map),
            pl.BlockSpec((None, None, MOE_TF, d), w_out_map),
        ],
        out_specs=pl.BlockSpec((MOE_TM, d), lambda b, f, be, nu: (b, 0)),
        scratch_shapes=[pltpu.VMEM((MOE_TM, d), BF16), pltpu.VMEM((MOE_TM, d), F32)],
    )
    return pl.pallas_call(
        _expert_kernel,
        grid_spec=grid_spec,
        out_shape=jax.ShapeDtypeStruct((nb * MOE_TM, d), F32),
        compiler_params=_cparams(("arbitrary", "arbitrary")),
        name="expert_ffn",
    )(block_e, n_used, xs, w_gate, w_up, w_down)


def _combine_kernel(first_row, with_bf16, pos1_ref, pos2_ref, yb_hbm, h_ref, w_ref, g_ref, b_ref, *rest):
    out_refs, (y1_ref, y2_ref, sem) = rest[:-3], rest[-3:]
    base = first_row + pl.program_id(0) * CMB_TM

    def row_copy(dst, r, src_row):
        return pltpu.make_async_copy(yb_hbm.at[pl.ds(src_row, 1)], dst.at[pl.ds(r, 1)], sem)

    def issue(r, carry):
        row_copy(y1_ref, r, pos1_ref[base + r]).start()
        row_copy(y2_ref, r, pos2_ref[base + r]).start()
        return carry

    lax.fori_loop(0, CMB_TM, issue, 0, unroll=8)

    def wait(r, carry):
        row_copy(y1_ref, r, 0).wait()
        row_copy(y2_ref, r, 0).wait()
        return carry

    lax.fori_loop(0, CMB_TM, wait, 0, unroll=8)
    w = w_ref[...]
    y = w[:, 0:1] * y1_ref[...] + w[:, 1:2] * y2_ref[...]
    out = _layer_norm(ALPHA * h_ref[...] + y, g_ref[...], b_ref[...])
    out_refs[0][...] = out
    if with_bf16:
        out_refs[1][...] = out.astype(BF16)


def _combine_ln(pos1, pos2, yb, hf, wts, g, b, layer, final):
    l, d = hf.shape
    first_block = PREFIX // CMB_TM if final else 0
    n_rows = l - first_block * CMB_TM
    src = lambda i, p1, p2: (i + first_block, 0)
    dst = lambda i, p1, p2: (i, 0)
    vec = pl.BlockSpec((None, 1, d), lambda i, p1, p2: (layer, 0, 0))
    out_specs = [pl.BlockSpec((CMB_TM, d), dst)]
    out_shape = [jax.ShapeDtypeStruct((n_rows, d), F32)]
    if not final:
        out_specs.append(pl.BlockSpec((CMB_TM, d), dst))
        out_shape.append(jax.ShapeDtypeStruct((n_rows, d), BF16))
    grid_spec = pltpu.PrefetchScalarGridSpec(
        num_scalar_prefetch=2,
        grid=(n_rows // CMB_TM,),
        in_specs=[
            pl.BlockSpec(memory_space=pl.ANY),
            pl.BlockSpec((CMB_TM, d), src),
            pl.BlockSpec((CMB_TM, TOP_K), src),
            vec, vec,
        ],
        out_specs=out_specs,
        scratch_shapes=[
            pltpu.VMEM((CMB_TM, d), F32),
            pltpu.VMEM((CMB_TM, d), F32),
            pltpu.SemaphoreType.DMA(()),
        ],
    )
    return pl.pallas_call(
        functools.partial(_combine_kernel, first_block * CMB_TM, not final),
        grid_spec=grid_spec,
        out_shape=out_shape,
        compiler_params=_cparams(("arbitrary",)),
        name="combine_ln",
    )(pos1, pos2, yb, hf, wts, g.reshape(DEPTH, 1, d), b.reshape(DEPTH, 1, d))


def _plan(route_i, counts_f):
    n = route_i.shape[1]
    nb = -(-(n * TOP_K + N_EXPERTS * (MOE_TM - 1)) // MOE_TM)
    counts = counts_f[:, 0].astype(jnp.int32)
    blocks_e = (counts + MOE_TM - 1) // MOE_TM
    bend = jnp.cumsum(blocks_e)
    base = (bend - blocks_e) * MOE_TM
    eids = jnp.arange(N_EXPERTS, dtype=jnp.int32)[:, None]
    base_of = lambda e: jnp.sum(jnp.where(e[None, :] == eids, base[:, None], 0), axis=0)
    pos1 = (route_i[2] + base_of(route_i[0])).astype(jnp.int32)
    pos2 = (route_i[3] + base_of(route_i[1])).astype(jnp.int32)
    n_used = bend[-1:].astype(jnp.int32)
    blocks = jnp.arange(nb, dtype=jnp.int32)
    block_e = jnp.minimum(jnp.searchsorted(bend, blocks, side='right'), N_EXPERTS - 1).astype(jnp.int32)
    partial = jnp.any(blocks[:, None] == (bend - 1)[None, :], axis=1)
    needs_fill = jnp.logical_or(partial, blocks >= n_used[0]).astype(jnp.int32)
    return pos1, pos2, needs_fill, block_e, n_used


def kernel(x, meta_tokens, ln_in_g, ln_in_b, w_in, b_forget, w_branch_sb, w_branch_fox, w_out,
           ln_mix_g, ln_mix_b, w_router, router_bias, w_gate, w_up, w_down, ln_ffn_g, ln_ffn_b):
    b, s, d = x.shape
    assert b == 1 and d == D_MODEL
    l = s + PREFIX
    prefix = jnp.concatenate([jnp.zeros((N_PAD, d), x.dtype), meta_tokens.astype(x.dtype)], axis=0)
    hf, hb = _ln_in(prefix, x[0], ln_in_g, ln_in_b)

    wr_t = w_router.astype(F32).T
    rb = jnp.broadcast_to(router_bias.astype(F32)[:, None], (N_EXPERTS, TM_LN))
    key_is_pad = (jnp.arange(l) < N_PAD)[None, :]
    for layer in range(DEPTH):
        w_gates = w_in[layer, :, G_COL0:].astype(BF16)
        wf_pad = jnp.pad(w_in[layer, :, F_COL0:G_COL0], ((0, 0), (0, LANE - N_HEADS)))
        bf_pad = jnp.pad(b_forget[layer], (0, LANE - N_HEADS)).reshape(1, LANE)

        qkv = _qkv_proj(hb, w_in, layer)
        gates = _gate_proj(hb, w_gates)
        log_f_cum = _forget_cumsum(hf, wf_pad, bf_pad)[:, :N_HEADS]
        key_bias = jnp.where(key_is_pad, NEG, -log_f_cum.T).reshape(N_HEADS, l // FOX_T, 1, FOX_T)
        y_sb = _sb_attention(qkv)
        y_fx = _fox_attention(qkv, key_bias)
        merged = _merge(y_sb, y_fx, gates, w_branch_sb, w_branch_fox, layer)
        hf, hb, hp, route_i, route_w, counts = _out_ln(merged, w_out, hf, ln_mix_g, ln_mix_b,
                                                       wr_t, rb, layer)

        pos1, pos2, zstart, block_e, n_used = _plan(route_i, counts)
        xs = _dispatch(pos1, pos2, zstart, hp, block_e.shape[0] * MOE_TM)
        yb = _expert_ffn(block_e, n_used, xs, w_gate, w_up, w_down, layer)
        final = layer == DEPTH - 1
        outs = _combine_ln(pos1, pos2, yb, hf, route_w[:TOP_K].T, ln_ffn_g, ln_ffn_b, layer, final)
        if not final:
            hf, hb = outs
    return outs[0][None]
```

```python
import functools

import jax
import jax.numpy as jnp
from jax import lax
from jax.experimental import pallas as pl
from jax.experimental.pallas import tpu as pltpu

F32 = jnp.float32
BF16 = jnp.bfloat16

D_MODEL = 2048
DEPTH = 2
HEAD_DIM = 128
N_HEADS = 8
WIDTH = N_HEADS * HEAD_DIM
N_META = 16
N_PAD = 112
PREFIX = N_PAD + N_META
N_EXPERTS = 16
N_GROUPS = 4
EXPERTS_PER_GROUP = N_EXPERTS // N_GROUPS
TOP_K = 2
EXPERT_FF = 1024
ALPHA = (2 * DEPTH) ** 0.25
LN_EPS = 1e-5
NEG = -1e30
QK_SCALE = HEAD_DIM ** -0.5
QKV_COLS = 6 * WIDTH
F_COL0 = QKV_COLS
G_COL0 = QKV_COLS + N_HEADS

SB_EXIT = -104.0

LANE = 128
VMEM_LIMIT = 56 * 1024 * 1024
TM_PROJ = 1664
TN_PROJ = 512
N_QKV_TILES = QKV_COLS // TN_PROJ
TM_LN = 640
TK_OUT = 512
SB_TQ = 128
SB_TK = 128
FOX_T = 640
FOX_GROUP = 4
MOE_TM = 512
MOE_TF = 256
CMB_TM = 128


def _cparams(sem):
    return pltpu.CompilerParams(dimension_semantics=sem, vmem_limit_bytes=VMEM_LIMIT)


def _layer_norm(x, g, b):
    mu = jnp.mean(x, axis=-1, keepdims=True)
    xc = x - mu
    var = jnp.mean(xc * xc, axis=-1, keepdims=True)
    return xc * lax.rsqrt(var + LN_EPS) * g + b


def _log_sigmoid(z):
    return jnp.minimum(z, 0.0) - jnp.log(1.0 + jnp.exp(-jnp.abs(z)))


def _ln_in_kernel(p_ref, x_ref, g_ref, b_ref, of_ref, ob_ref):
    src = jnp.where(pl.program_id(0) == 0, p_ref[...], x_ref[...])
    y = _layer_norm(src, g_ref[...], b_ref[...])
    of_ref[...] = y
    ob_ref[...] = y.astype(BF16)


def _ln_in(prefix, x, g, b):
    s, d = x.shape
    l = s + PREFIX
    row = pl.BlockSpec((PREFIX, d), lambda i: (i, 0))
    vec = pl.BlockSpec((1, d), lambda i: (0, 0))
    return pl.pallas_call(
        _ln_in_kernel,
        grid=(l // PREFIX,),
        in_specs=[
            pl.BlockSpec((PREFIX, d), lambda i: (0, 0)),
            pl.BlockSpec((PREFIX, d), lambda i: (jnp.maximum(i - 1, 0), 0)),
            vec, vec,
        ],
        out_specs=[row, row],
        out_shape=[jax.ShapeDtypeStruct((l, d), F32), jax.ShapeDtypeStruct((l, d), BF16)],
        compiler_params=_cparams(("arbitrary",)),
        name="ln_in",
    )(prefix, x, g.reshape(1, d), b.reshape(1, d))


def _in_proj_kernel(layer, x_ref, wt_hbm, o_ref, stage_ref, wb_ref, sem):
    j = pl.program_id(0)
    i = pl.program_id(1)

    def w_copy(tile, slot):
        row0 = jnp.where(tile < N_QKV_TILES, tile * TN_PROJ, G_COL0 + (tile - N_QKV_TILES) * TN_PROJ)
        row0 = pl.multiple_of(row0, 8)
        return pltpu.make_async_copy(wt_hbm.at[layer, pl.ds(row0, TN_PROJ), :],
                                     stage_ref.at[slot], sem.at[slot])

    @pl.when(i == 0)
    def _():
        @pl.when(j == 0)
        def _():
            w_copy(0, 0).start()

        @pl.when(j + 1 < pl.num_programs(0))
        def _():
            w_copy(j + 1, (j + 1) % 2).start()

        w_copy(j, j % 2).wait()
        wb_ref[...] = stage_ref[j % 2].astype(BF16)

    acc = lax.dot_general(x_ref[...], wb_ref[...], (((1,), (1,)), ((), ())),
                          preferred_element_type=F32)

    @pl.when(j < N_QKV_TILES)
    def _():
        part = j % (3 * WIDTH // TN_PROJ)
        is_q = part < WIDTH // TN_PROJ
        is_v = part >= 2 * WIDTH // TN_PROJ
        y = acc * jnp.where(is_q, QK_SCALE, 1.0)
        rows = i * TM_PROJ + lax.broadcasted_iota(jnp.int32, acc.shape, 0)
        o_ref[...] = jnp.where(jnp.logical_and(is_v, rows < N_PAD), 0.0, y).astype(BF16)

    @pl.when(j >= N_QKV_TILES)
    def _():
        o_ref[...] = jax.nn.sigmoid(acc).astype(BF16)


def _in_proj(hb, w_in_t, layer):
    l, d = hb.shape
    n_tiles = N_QKV_TILES + 2 * D_MODEL // TN_PROJ
    return pl.pallas_call(
        functools.partial(_in_proj_kernel, layer),
        grid=(n_tiles, l // TM_PROJ),
        in_specs=[
            pl.BlockSpec((TM_PROJ, d), lambda j, i: (i, 0)),
            pl.BlockSpec(memory_space=pl.ANY),
        ],
        out_specs=pl.BlockSpec((TM_PROJ, TN_PROJ), lambda j, i: (i, j)),
        out_shape=jax.ShapeDtypeStruct((l, n_tiles * TN_PROJ), BF16),
        scratch_shapes=[pltpu.VMEM((2, TN_PROJ, d), F32), pltpu.VMEM((TN_PROJ, d), BF16),
                        pltpu.SemaphoreType.DMA((2,))],
        compiler_params=_cparams(("arbitrary", "arbitrary")),
        name="in_proj",
    )(hb, w_in_t)


def _forget_kernel(h_ref, w_ref, b_ref, o_ref, carry_ref):
    i = pl.program_id(0)

    @pl.when(i == 0)
    def _():
        carry_ref[...] = jnp.zeros_like(carry_ref)

    f = lax.dot_general(w_ref[...], h_ref[...], (((1,), (1,)), ((), ())),
                        precision=lax.Precision.HIGHEST, preferred_element_type=F32) + b_ref[...]
    pos = i * TM_LN + lax.broadcasted_iota(jnp.int32, f.shape, 1)
    lf = jnp.where(pos >= N_PAD, _log_sigmoid(f), 0.0)
    r = lax.broadcasted_iota(jnp.int32, (TM_LN, TM_LN), 0)
    c = lax.broadcasted_iota(jnp.int32, (TM_LN, TM_LN), 1)
    upto = jnp.where(r <= c, 1.0, 0.0).astype(F32)
    cs = jnp.dot(lf, upto, precision=lax.Precision.HIGHEST, preferred_element_type=F32) + carry_ref[:, 0:1]
    o_ref[...] = jnp.where(pos >= N_PAD, -cs, NEG)
    carry_ref[...] = jnp.broadcast_to(cs[:, TM_LN - 1:TM_LN], carry_ref.shape)


def _forget_bias(hf, w_in_t, bf, layer):
    l, d = hf.shape
    return pl.pallas_call(
        _forget_kernel,
        grid=(l // TM_LN,),
        in_specs=[
            pl.BlockSpec((TM_LN, d), lambda i: (i, 0)),
            pl.BlockSpec((None, N_HEADS, d), lambda i: (layer, F_COL0 // N_HEADS, 0)),
            pl.BlockSpec((N_HEADS, TM_LN), lambda i: (0, 0)),
        ],
        out_specs=pl.BlockSpec((N_HEADS, TM_LN), lambda i: (0, i)),
        out_shape=jax.ShapeDtypeStruct((N_HEADS, l), F32),
        scratch_shapes=[pltpu.VMEM((N_HEADS, LANE), F32)],
        compiler_params=_cparams(("arbitrary",)),
        name="forget_bias",
    )(hf, w_in_t, bf)


def _sb_kernel(q_ref, k_ref, v_ref, o_ref, run_ref, acc_ref):
    i = pl.program_id(0)
    tq, tk = SB_TQ, SB_TK
    r = lax.broadcasted_iota(jnp.int32, (2 * tk, 2 * tk), 0)
    c = lax.broadcasted_iota(jnp.int32, (2 * tk, 2 * tk), 1)
    suffix = jnp.where(jnp.logical_or(c >= tk, (r % tk) > c), 1.0, 0.0).astype(BF16)
    row = lax.broadcasted_iota(jnp.int32, (tq, tk), 0)
    col = lax.broadcasted_iota(jnp.int32, (tq, tk), 1)
    strictly_causal = col < row

    def block(j, diagonal):
        start = pl.multiple_of(j * tk, tk)
        worst = None
        for hd in range(N_HEADS):
            cols = slice(hd * HEAD_DIM, (hd + 1) * HEAD_DIM)
            q = q_ref[:, cols]
            kj = k_ref[pl.ds(start, tk), cols]
            vj = v_ref[pl.ds(start, tk), cols]
            z = lax.dot_general(q, kj, (((1,), (1,)), ((), ())), preferred_element_type=F32)
            log_beta = _log_sigmoid(z)
            log_keep = log_beta - z
            if diagonal:
                log_keep = jnp.where(strictly_causal, log_keep, 0.0)
            hi = log_keep.astype(BF16)
            lo = (log_keep - hi.astype(F32)).astype(BF16)
            sums = jnp.dot(jnp.concatenate([hi, lo], axis=1), suffix, preferred_element_type=F32)
            if diagonal:
                w = jnp.where(strictly_causal, jnp.exp(log_beta + sums[:, :tk]), 0.0)
                run = sums[:, tk:]
                acc_ref[:, cols] = jnp.dot(w.astype(BF16), vj, preferred_element_type=F32)
            else:
                run = run_ref[:, cols]
                w = jnp.exp(log_beta + sums[:, :tk] + run)
                run = run + sums[:, tk:]
                acc_ref[:, cols] += jnp.dot(w.astype(BF16), vj, preferred_element_type=F32)
            run_ref[:, cols] = run
            worst = run if worst is None else jnp.maximum(worst, run)
        return jnp.max(worst)

    worst = block(i, True)

    def cond(s):
        j, worst = s
        return jnp.logical_and(j >= 0, worst > SB_EXIT)

    def body(s):
        j, _ = s
        return j - 1, block(j, False)

    lax.while_loop(cond, body, (i - 1, worst))
    o_ref[...] = acc_ref[...].astype(BF16)


def _sb_attention(proj):
    l = proj.shape[0]
    resident = lambda blk: pl.BlockSpec((l, WIDTH), lambda i: (0, blk), pipeline_mode=pl.Buffered(1))
    return pl.pallas_call(
        _sb_kernel,
        grid=(l // SB_TQ,),
        in_specs=[pl.BlockSpec((SB_TQ, WIDTH), lambda i: (i, 0)), resident(1), resident(2)],
        out_specs=pl.BlockSpec((SB_TQ, WIDTH), lambda i: (i, 0)),
        out_shape=jax.ShapeDtypeStruct((l, WIDTH), BF16),
        scratch_shapes=[pltpu.VMEM((SB_TQ, WIDTH), F32), pltpu.VMEM((SB_TQ, WIDTH), F32)],
        compiler_params=_cparams(("arbitrary",)),
        name="sb_attention",
    )(proj, proj, proj)


def _fox_kernel(q_ref, k_ref, v_ref, b_ref, o_ref, vaug_ref):
    i = pl.program_id(1)
    t = FOX_T

    @pl.when(i == 0)
    def _():
        lane = lax.broadcasted_iota(jnp.int32, (vaug_ref.shape[1], HEAD_DIM), 1)
        ones_col = jnp.where(lane == 0, 1.0, 0.0).astype(BF16)
        for hd in range(FOX_GROUP):
            vaug_ref[hd, :, :HEAD_DIM] = v_ref[:, hd * HEAD_DIM:(hd + 1) * HEAD_DIM]
            vaug_ref[hd, :, HEAD_DIM:] = ones_col

    row = lax.broadcasted_iota(jnp.int32, (t, t), 0)
    col = lax.broadcasted_iota(jnp.int32, (t, t), 1)
    causal = col <= row

    def block(j, state, diagonal):
        start = pl.multiple_of(j * t, t)
        out = []
        for hd, (m, acc) in enumerate(state):
            cols = slice(hd * HEAD_DIM, (hd + 1) * HEAD_DIM)
            q = q_ref[:, cols]
            kj = k_ref[pl.ds(start, t), cols]
            vj = vaug_ref[hd, pl.ds(start, t), :]
            s = lax.dot_general(q, kj, (((1,), (1,)), ((), ())), preferred_element_type=F32)
            s = s + b_ref[hd, j]
            if diagonal:
                s = jnp.where(causal, s, NEG)
            m_new = jnp.maximum(m, jnp.max(s, axis=1, keepdims=True))
            p = jnp.exp(s - m_new).astype(BF16)
            acc = jnp.exp(m - m_new) * acc + jnp.dot(p, vj, preferred_element_type=F32)
            out.append((m_new, acc))
        return tuple(out)

    init = tuple((jnp.full((t, 1), NEG, F32), jnp.zeros((t, 2 * HEAD_DIM), F32))
                 for _ in range(FOX_GROUP))
    state = block(i, init, True)
    state = lax.fori_loop(0, i, lambda j, s: block(j, s, False), state)
    for hd, (_, acc) in enumerate(state):
        o_ref[:, hd * HEAD_DIM:(hd + 1) * HEAD_DIM] = (
            acc[:, :HEAD_DIM] / acc[:, HEAD_DIM:HEAD_DIM + 1]).astype(BF16)


def _fox_attention(proj, key_bias):
    l = proj.shape[0]
    gw = FOX_GROUP * HEAD_DIM
    first = 3 * WIDTH // gw
    per = WIDTH // gw
    nkb = l // FOX_T
    return pl.pallas_call(
        _fox_kernel,
        grid=(N_HEADS // FOX_GROUP, l // FOX_T),
        in_specs=[
            pl.BlockSpec((FOX_T, gw), lambda g, i: (i, first + g)),
            pl.BlockSpec((l, gw), lambda g, i: (0, first + per + g), pipeline_mode=pl.Buffered(1)),
            pl.BlockSpec((l, gw), lambda g, i: (0, first + 2 * per + g), pipeline_mode=pl.Buffered(1)),
            pl.BlockSpec((FOX_GROUP, nkb, 1, FOX_T), lambda g, i: (g, 0, 0, 0)),
        ],
        out_specs=pl.BlockSpec((FOX_T, gw), lambda g, i: (i, g)),
        out_shape=jax.ShapeDtypeStruct((l, WIDTH), BF16),
        scratch_shapes=[pltpu.VMEM((FOX_GROUP, l, 2 * HEAD_DIM), BF16)],
        compiler_params=_cparams(("arbitrary", "arbitrary")),
        name="fox_attention",
    )(proj, proj, proj, key_bias)


def _merge_kernel(ys_ref, yf_ref, gs_ref, gf_ref, ws_ref, wf_ref, o_ref, wsb_ref, wfb_ref):
    i = pl.program_id(1)

    @pl.when(i == 0)
    def _():
        wsb_ref[...] = ws_ref[...].astype(BF16)
        wfb_ref[...] = wf_ref[...].astype(BF16)

    a = jnp.dot(ys_ref[...], wsb_ref[...], preferred_element_type=F32)
    b = jnp.dot(yf_ref[...], wfb_ref[...], preferred_element_type=F32)
    o_ref[...] = (gs_ref[...].astype(F32) * a + gf_ref[...].astype(F32) * b).astype(BF16)


def _merge(y_sb, y_fx, proj, w_bsb, w_bfx, layer):
    l = y_sb.shape[0]
    d = D_MODEL
    nj = d // TN_PROJ
    act = pl.BlockSpec((TM_PROJ, WIDTH), lambda j, i: (i, 0))
    wspec = pl.BlockSpec((None, WIDTH, TN_PROJ), lambda j, i: (layer, 0, j))
    return pl.pallas_call(
        _merge_kernel,
        grid=(nj, l // TM_PROJ),
        in_specs=[
            act, act,
            pl.BlockSpec((TM_PROJ, TN_PROJ), lambda j, i: (i, N_QKV_TILES + j)),
            pl.BlockSpec((TM_PROJ, TN_PROJ), lambda j, i: (i, N_QKV_TILES + nj + j)),
            wspec, wspec,
        ],
        out_specs=pl.BlockSpec((TM_PROJ, TN_PROJ), lambda j, i: (i, j)),
        out_shape=jax.ShapeDtypeStruct((l, d), BF16),
        scratch_shapes=[pltpu.VMEM((WIDTH, TN_PROJ), BF16), pltpu.VMEM((WIDTH, TN_PROJ), BF16)],
        compiler_params=_cparams(("arbitrary", "arbitrary")),
        name="merge",
    )(y_sb, y_fx, proj, proj, w_bsb, w_bfx)


def _pick(index, values):
    out = values[0]
    for k in range(1, len(values)):
        out = jnp.where(index == k, values[k], out)
    return out


def _grouped_top2(sel, aff):
    srow = [sel[e:e + 1, :] for e in range(N_EXPERTS)]
    arow = [aff[e:e + 1, :] for e in range(N_EXPERTS)]
    n = EXPERTS_PER_GROUP
    score = []
    for g in range(N_GROUPS):
        v = srow[g * n:(g + 1) * n]
        pair = [v[a] + v[b] for a in range(n) for b in range(a + 1, n)]
        score.append(functools.reduce(jnp.maximum, pair))
    group = jnp.zeros_like(score[0], dtype=jnp.int32)
    best = score[0]
    for g in range(1, N_GROUPS):
        better = score[g] > best
        group = jnp.where(better, g, group)
        best = jnp.where(better, score[g], best)
    v = [_pick(group, [srow[g * n + k] for g in range(N_GROUPS)]) for k in range(n)]
    a = [_pick(group, [arow[g * n + k] for g in range(N_GROUPS)]) for k in range(n)]
    i1 = jnp.zeros_like(group)
    m1 = v[0]
    for k in range(1, n):
        better = v[k] > m1
        i1 = jnp.where(better, k, i1)
        m1 = jnp.where(better, v[k], m1)
    first_is_0 = i1 == 0
    i2 = jnp.where(first_is_0, 1, 0)
    m2 = jnp.where(first_is_0, v[1], v[0])
    for k in range(1, n):
        better = jnp.logical_and(i1 != k, v[k] > m2)
        i2 = jnp.where(better, k, i2)
        m2 = jnp.where(better, v[k], m2)
    a1 = _pick(i1, a)
    a2 = _pick(i2, a)
    tot = a1 + a2
    return group * n + i1, group * n + i2, a1 / tot, a2 / tot


def _out_ln_kernel(x_ref, w_ref, h_ref, g_ref, b_ref, wr_ref, rb_ref,
                   of_ref, ob_ref, hp_ref, ri_ref, rw_ref, cnt_ref, acc_ref):
    i = pl.program_id(0)
    k = pl.program_id(1)

    @pl.when(k == 0)
    def _():
        acc_ref[...] = jnp.zeros_like(acc_ref)

    @pl.when(jnp.logical_and(i == 0, k == 0))
    def _():
        cnt_ref[...] = jnp.zeros_like(cnt_ref)

    acc_ref[...] += jnp.dot(x_ref[...], w_ref[...].astype(BF16), preferred_element_type=F32)

    @pl.when(k == pl.num_programs(1) - 1)
    def _():
        y = _layer_norm(ALPHA * h_ref[...] + acc_ref[...], g_ref[...], b_ref[...])
        of_ref[...] = y
        yb = y.astype(BF16)
        ob_ref[...] = yb
        bits = pltpu.bitcast(yb.astype(F32), jnp.uint32)
        half = D_MODEL // 2
        hp_ref[...] = (bits[:, :half] >> 16) | (bits[:, half:] & jnp.uint32(0xFFFF0000))

        logits = lax.dot_general(wr_ref[...], y, (((1,), (1,)), ((), ())),
                                 precision=lax.Precision.HIGHEST, preferred_element_type=F32)
        aff = jax.nn.sigmoid(logits)
        e1, e2, w1, w2 = _grouped_top2(aff + rb_ref[...], aff)
        t = TM_LN
        eid = lax.broadcasted_iota(jnp.int32, (N_EXPERTS, t), 0)
        hit1 = eid == e1
        hit2 = eid == e2
        onehot = jnp.where(jnp.logical_or(hit1, hit2), 1.0, 0.0)
        r = lax.broadcasted_iota(jnp.int32, (t, t), 0)
        c = lax.broadcasted_iota(jnp.int32, (t, t), 1)
        before = jnp.where(r < c, 1.0, 0.0).astype(BF16)
        seen = jnp.dot(onehot.astype(BF16), before, preferred_element_type=F32) + cnt_ref[:, 0:1]
        rank1 = jnp.sum(jnp.where(hit1, seen, 0.0), axis=0, keepdims=True)
        rank2 = jnp.sum(jnp.where(hit2, seen, 0.0), axis=0, keepdims=True)
        ri_ref[0:1, :] = e1
        ri_ref[1:2, :] = e2
        ri_ref[2:3, :] = rank1.astype(jnp.int32)
        ri_ref[3:4, :] = rank2.astype(jnp.int32)
        ri_ref[4:8, :] = jnp.zeros((4, t), jnp.int32)
        rw_ref[0:1, :] = w1
        rw_ref[1:2, :] = w2
        rw_ref[2:8, :] = jnp.zeros((6, t), F32)
        cnt_ref[...] += jnp.sum(onehot, axis=1, keepdims=True)


def _out_ln(merged, w_out, hf, g, b, wr_t, rb, layer):
    l, d = hf.shape
    row = pl.BlockSpec((TM_LN, d), lambda i, k: (i, 0))
    vec = pl.BlockSpec((None, 1, d), lambda i, k: (layer, 0, 0))
    tok = pl.BlockSpec((8, TM_LN), lambda i, k: (0, i))
    return pl.pallas_call(
        _out_ln_kernel,
        grid=(l // TM_LN, d // TK_OUT),
        in_specs=[
            pl.BlockSpec((TM_LN, TK_OUT), lambda i, k: (i, k)),
            pl.BlockSpec((None, TK_OUT, d), lambda i, k: (layer, k, 0)),
            row, vec, vec,
            pl.BlockSpec((N_EXPERTS, d), lambda i, k: (0, 0)),
            pl.BlockSpec((N_EXPERTS, TM_LN), lambda i, k: (0, 0)),
        ],
        out_specs=[row, row, pl.BlockSpec((TM_LN, d // 2), lambda i, k: (i, 0)), tok, tok,
                   pl.BlockSpec((N_EXPERTS, LANE), lambda i, k: (0, 0))],
        out_shape=[jax.ShapeDtypeStruct((l, d), F32), jax.ShapeDtypeStruct((l, d), BF16),
                   jax.ShapeDtypeStruct((l, d // 2), jnp.uint32),
                   jax.ShapeDtypeStruct((8, l), jnp.int32), jax.ShapeDtypeStruct((8, l), F32),
                   jax.ShapeDtypeStruct((N_EXPERTS, LANE), F32)],
        scratch_shapes=[pltpu.VMEM((TM_LN, d), F32)],
        compiler_params=_cparams(("arbitrary", "arbitrary")),
        name="out_ln",
    )(merged, w_out, hf, g.reshape(DEPTH, 1, d), b.reshape(DEPTH, 1, d), wr_t, rb)


def _invert_kernel(pos1_ref, pos2_ref, tok_ref):
    def clear(r, carry):
        tok_ref[r] = 0
        return carry

    lax.fori_loop(0, tok_ref.shape[0], clear, 0, unroll=8)

    def mark(t, carry):
        tok_ref[pos1_ref[t]] = t
        tok_ref[pos2_ref[t]] = t
        return carry

    lax.fori_loop(0, pos1_ref.shape[0], mark, 0, unroll=8)


def _invert(pos1, pos2, n_rows):
    smem = pl.BlockSpec(memory_space=pltpu.SMEM)
    return pl.pallas_call(
        _invert_kernel,
        in_specs=[smem, smem],
        out_specs=smem,
        out_shape=jax.ShapeDtypeStruct((n_rows,), jnp.int32),
        name="invert",
    )(pos1, pos2)


def _expert_kernel(be_ref, nu_ref, tok_ref, hp_hbm, wg_ref, wu_ref, wd_ref, o_ref,
                   xs_ref, xb_ref, acc_ref, sem):
    blk = pl.program_id(0)
    f = pl.program_id(1)
    n_used = nu_ref[0]
    used = blk < n_used
    slot = blk % 2
    rows_per_step = MOE_TM // (EXPERT_FF // MOE_TF)

    def row_copy(s, r, src_row):
        return pltpu.make_async_copy(hp_hbm.at[pl.ds(src_row, 1)], xs_ref.at[s, pl.ds(r, 1)], sem.at[s])

    @pl.when(jnp.logical_and(blk == 0, f == 0))
    def _():
        def issue(r, carry):
            row_copy(0, r, tok_ref[r]).start()
            return carry

        lax.fori_loop(0, MOE_TM, issue, 0, unroll=8)

    @pl.when(jnp.logical_and(blk <= n_used, f == 0))
    def _():
        def wait(r, carry):
            row_copy(slot, 0, 0).wait()
            return carry

        lax.fori_loop(0, MOE_TM, wait, 0, unroll=8)

    @pl.when(jnp.logical_and(used, f == 0))
    def _():
        words = xs_ref[slot]
        half = D_MODEL // 2
        xb_ref[:, :half] = pltpu.bitcast(words << 16, F32).astype(BF16)
        xb_ref[:, half:] = pltpu.bitcast(words & jnp.uint32(0xFFFF0000), F32).astype(BF16)
        acc_ref[...] = jnp.zeros_like(acc_ref)

    @pl.when(used)
    def _():
        base = (blk + 1) * MOE_TM + f * rows_per_step
        for r in range(rows_per_step):
            row_copy(1 - slot, f * rows_per_step + r, tok_ref[base + r]).start()
        x = xb_ref[...]
        g = jnp.dot(x, wg_ref[...].astype(BF16), preferred_element_type=F32)
        u = jnp.dot(x, wu_ref[...].astype(BF16), preferred_element_type=F32)
        a = (g * jax.nn.sigmoid(g) * u).astype(BF16)
        acc_ref[...] += jnp.dot(a, wd_ref[...].astype(BF16), preferred_element_type=F32)

    @pl.when(f == pl.num_programs(1) - 1)
    def _():
        o_ref[...] = jnp.where(used, acc_ref[...], 0.0)


def _expert_ffn(block_e, n_used, row_tok, hp, w_gate, w_up, w_down, layer):
    d = D_MODEL
    nb = block_e.shape[0]
    nf = EXPERT_FF // MOE_TF

    def w_in_map(b, f, be, nu, tok):
        return (layer, be[b], 0, jnp.where(b < nu[0], f, nf - 1))

    def w_out_map(b, f, be, nu, tok):
        return (layer, be[b], jnp.where(b < nu[0], f, nf - 1), 0)

    grid_spec = pltpu.PrefetchScalarGridSpec(
        num_scalar_prefetch=3,
        grid=(nb, nf),
        in_specs=[
            pl.BlockSpec(memory_space=pl.ANY),
            pl.BlockSpec((None, None, d, MOE_TF), w_in_map),
            pl.BlockSpec((None, None, d, MOE_TF), w_in_map),
            pl.BlockSpec((None, None, MOE_TF, d), w_out_map),
        ],
        out_specs=pl.BlockSpec((MOE_TM, d), lambda b, f, be, nu, tok: (b, 0)),
        scratch_shapes=[pltpu.VMEM((2, MOE_TM, d // 2), jnp.uint32), pltpu.VMEM((MOE_TM, d), BF16),
                        pltpu.VMEM((MOE_TM, d), F32), pltpu.SemaphoreType.DMA((2,))],
    )
    return pl.pallas_call(
        _expert_kernel,
        grid_spec=grid_spec,
        out_shape=jax.ShapeDtypeStruct((nb * MOE_TM, d), F32),
        compiler_params=_cparams(("arbitrary", "arbitrary")),
        name="expert_ffn",
    )(block_e, n_used, row_tok, hp, w_gate, w_up, w_down)


def _combine_kernel(first_row, with_bf16, pos1_ref, pos2_ref, yb_hbm, h_ref, w_ref, g_ref, b_ref, *rest):
    out_refs, (y1_ref, y2_ref, sem) = rest[:-3], rest[-3:]
    base = first_row + pl.program_id(0) * CMB_TM

    def row_copy(dst, r, src_row):
        return pltpu.make_async_copy(yb_hbm.at[pl.ds(src_row, 1)], dst.at[pl.ds(r, 1)], sem)

    def issue(r, carry):
        row_copy(y1_ref, r, pos1_ref[base + r]).start()
        row_copy(y2_ref, r, pos2_ref[base + r]).start()
        return carry

    lax.fori_loop(0, CMB_TM, issue, 0, unroll=8)

    def wait(r, carry):
        row_copy(y1_ref, r, 0).wait()
        row_copy(y2_ref, r, 0).wait()
        return carry

    lax.fori_loop(0, CMB_TM, wait, 0, unroll=8)
    w = w_ref[...]
    y = w[:, 0:1] * y1_ref[...] + w[:, 1:2] * y2_ref[...]
    out = _layer_norm(ALPHA * h_ref[...] + y, g_ref[...], b_ref[...])
    out_refs[0][...] = out
    if with_bf16:
        out_refs[1][...] = out.astype(BF16)


def _combine_ln(pos1, pos2, yb, hf, wts, g, b, layer, final):
    l, d = hf.shape
    first_block = PREFIX // CMB_TM if final else 0
    n_rows = l - first_block * CMB_TM
    src = lambda i, p1, p2: (i + first_block, 0)
    dst = lambda i, p1, p2: (i, 0)
    vec = pl.BlockSpec((None, 1, d), lambda i, p1, p2: (layer, 0, 0))
    out_specs = [pl.BlockSpec((CMB_TM, d), dst)]
    out_shape = [jax.ShapeDtypeStruct((n_rows, d), F32)]
    if not final:
        out_specs.append(pl.BlockSpec((CMB_TM, d), dst))
        out_shape.append(jax.ShapeDtypeStruct((n_rows, d), BF16))
    grid_spec = pltpu.PrefetchScalarGridSpec(
        num_scalar_prefetch=2,
        grid=(n_rows // CMB_TM,),
        in_specs=[
            pl.BlockSpec(memory_space=pl.ANY),
            pl.BlockSpec((CMB_TM, d), src),
            pl.BlockSpec((CMB_TM, TOP_K), src),
            vec, vec,
        ],
        out_specs=out_specs,
        scratch_shapes=[
            pltpu.VMEM((CMB_TM, d), F32),
            pltpu.VMEM((CMB_TM, d), F32),
            pltpu.SemaphoreType.DMA(()),
        ],
    )
    return pl.pallas_call(
        functools.partial(_combine_kernel, first_block * CMB_TM, not final),
        grid_spec=grid_spec,
        out_shape=out_shape,
        compiler_params=_cparams(("arbitrary",)),
        name="combine_ln",
    )(pos1, pos2, yb, hf, wts, g.reshape(DEPTH, 1, d), b.reshape(DEPTH, 1, d))


def _plan(route_i, counts_f):
    n = route_i.shape[1]
    nb = -(-(n * TOP_K + N_EXPERTS * (MOE_TM - 1)) // MOE_TM)
    counts = counts_f[:, 0].astype(jnp.int32)
    blocks_e = (counts + MOE_TM - 1) // MOE_TM
    bend = jnp.cumsum(blocks_e)
    base = (bend - blocks_e) * MOE_TM
    eids = jnp.arange(N_EXPERTS, dtype=jnp.int32)[:, None]
    base_of = lambda e: jnp.sum(jnp.where(e[None, :] == eids, base[:, None], 0), axis=0)
    pos1 = (route_i[2] + base_of(route_i[0])).astype(jnp.int32)
    pos2 = (route_i[3] + base_of(route_i[1])).astype(jnp.int32)
    n_used = bend[-1:].astype(jnp.int32)
    blocks = jnp.arange(nb + 1, dtype=jnp.int32)
    block_e = jnp.minimum(jnp.searchsorted(bend, blocks, side='right'), N_EXPERTS - 1).astype(jnp.int32)
    return pos1, pos2, block_e, n_used


def kernel(x, meta_tokens, ln_in_g, ln_in_b, w_in, b_forget, w_branch_sb, w_branch_fox, w_out,
           ln_mix_g, ln_mix_b, w_router, router_bias, w_gate, w_up, w_down, ln_ffn_g, ln_ffn_b):
    b, s, d = x.shape
    assert b == 1 and d == D_MODEL
    l = s + PREFIX
    prefix = jnp.concatenate([jnp.zeros((N_PAD, d), x.dtype), meta_tokens.astype(x.dtype)], axis=0)
    hf, hb = _ln_in(prefix, x[0], ln_in_g, ln_in_b)

    w_in_t = jnp.swapaxes(w_in, 1, 2)
    wr_t = w_router.astype(F32).T
    rb = jnp.broadcast_to(router_bias.astype(F32)[:, None], (N_EXPERTS, TM_LN))
    for layer in range(DEPTH):
        proj = _in_proj(hb, w_in_t, layer)
        bf = jnp.broadcast_to(b_forget[layer].astype(F32)[:, None], (N_HEADS, TM_LN))
        key_bias = _forget_bias(hf, w_in_t, bf, layer).reshape(N_HEADS, l // FOX_T, 1, FOX_T)
        y_sb = _sb_attention(proj)
        y_fx = _fox_attention(proj, key_bias)
        merged = _merge(y_sb, y_fx, proj, w_branch_sb, w_branch_fox, layer)
        hf, hb, hp, route_i, route_w, counts = _out_ln(merged, w_out, hf, ln_mix_g, ln_mix_b,
                                                       wr_t, rb, layer)

        pos1, pos2, block_e, n_used = _plan(route_i, counts)
        row_tok = _invert(pos1, pos2, block_e.shape[0] * MOE_TM)
        yb = _expert_ffn(block_e, n_used, row_tok, hp, w_gate, w_up, w_down, layer)
        final = layer == DEPTH - 1
        outs = _combine_ln(pos1, pos2, yb, hf, route_w[:TOP_K].T, ln_ffn_g, ln_ffn_b, layer, final)
        if not final:
            hf, hb = outs
    return outs[0][None]
```

```python
import functools

import jax
import jax.numpy as jnp
from jax import lax
from jax.experimental import pallas as pl
from jax.experimental.pallas import tpu as pltpu

F32 = jnp.float32
BF16 = jnp.bfloat16

D_MODEL = 2048
DEPTH = 2
HEAD_DIM = 128
N_HEADS = 8
WIDTH = N_HEADS * HEAD_DIM
N_META = 16
N_PAD = 112
PREFIX = N_PAD + N_META
N_EXPERTS = 16
N_GROUPS = 4
EXPERTS_PER_GROUP = N_EXPERTS // N_GROUPS
TOP_K = 2
EXPERT_FF = 1024
ALPHA = (2 * DEPTH) ** 0.25
LN_EPS = 1e-5
NEG = -1e30
QK_SCALE = HEAD_DIM ** -0.5
QKV_COLS = 6 * WIDTH
F_COL0 = QKV_COLS
G_COL0 = QKV_COLS + N_HEADS

SB_EXIT = -104.0

LANE = 128
VMEM_LIMIT = 56 * 1024 * 1024
TM_PROJ = 1664
TN_PROJ = 512
N_QKV_TILES = QKV_COLS // TN_PROJ
TM_LN = 640
TK_OUT = 512
SB_TQ = 128
SB_TK = 128
FOX_T = 640
FOX_GROUP = 4
MOE_TM = 1024
MOE_HALF = MOE_TM // 2
MOE_TF = 256
CMB_TM = 128


def _cparams(sem):
    return pltpu.CompilerParams(dimension_semantics=sem, vmem_limit_bytes=VMEM_LIMIT)


def _layer_norm(x, g, b):
    mu = jnp.mean(x, axis=-1, keepdims=True)
    xc = x - mu
    var = jnp.mean(xc * xc, axis=-1, keepdims=True)
    return xc * lax.rsqrt(var + LN_EPS) * g + b


def _log_sigmoid(z):
    return jnp.minimum(z, 0.0) - jnp.log(1.0 + jnp.exp(-jnp.abs(z)))


def _ln_in_kernel(p_ref, x_ref, g_ref, b_ref, of_ref, ob_ref):
    src = jnp.where(pl.program_id(0) == 0, p_ref[...], x_ref[...])
    y = _layer_norm(src, g_ref[...], b_ref[...])
    of_ref[...] = y
    ob_ref[...] = y.astype(BF16)


def _ln_in(prefix, x, g, b):
    s, d = x.shape
    l = s + PREFIX
    row = pl.BlockSpec((PREFIX, d), lambda i: (i, 0))
    vec = pl.BlockSpec((1, d), lambda i: (0, 0))
    return pl.pallas_call(
        _ln_in_kernel,
        grid=(l // PREFIX,),
        in_specs=[
            pl.BlockSpec((PREFIX, d), lambda i: (0, 0)),
            pl.BlockSpec((PREFIX, d), lambda i: (jnp.maximum(i - 1, 0), 0)),
            vec, vec,
        ],
        out_specs=[row, row],
        out_shape=[jax.ShapeDtypeStruct((l, d), F32), jax.ShapeDtypeStruct((l, d), BF16)],
        compiler_params=_cparams(("arbitrary",)),
        name="ln_in",
    )(prefix, x, g.reshape(1, d), b.reshape(1, d))


def _in_proj_kernel(layer, x_ref, wt_hbm, o_ref, stage_ref, wb_ref, sem):
    j = pl.program_id(0)
    i = pl.program_id(1)

    def w_copy(tile, slot):
        row0 = jnp.where(tile < N_QKV_TILES, tile * TN_PROJ, G_COL0 + (tile - N_QKV_TILES) * TN_PROJ)
        row0 = pl.multiple_of(row0, 8)
        return pltpu.make_async_copy(wt_hbm.at[layer, pl.ds(row0, TN_PROJ), :],
                                     stage_ref.at[slot], sem.at[slot])

    @pl.when(i == 0)
    def _():
        @pl.when(j == 0)
        def _():
            w_copy(0, 0).start()

        @pl.when(j + 1 < pl.num_programs(0))
        def _():
            w_copy(j + 1, (j + 1) % 2).start()

        w_copy(j, j % 2).wait()
        wb_ref[...] = stage_ref[j % 2].astype(BF16)

    acc = lax.dot_general(x_ref[...], wb_ref[...], (((1,), (1,)), ((), ())),
                          preferred_element_type=F32)

    @pl.when(j < N_QKV_TILES)
    def _():
        part = j % (3 * WIDTH // TN_PROJ)
        is_q = part < WIDTH // TN_PROJ
        is_v = part >= 2 * WIDTH // TN_PROJ
        y = acc * jnp.where(is_q, QK_SCALE, 1.0)
        rows = i * TM_PROJ + lax.broadcasted_iota(jnp.int32, acc.shape, 0)
        o_ref[...] = jnp.where(jnp.logical_and(is_v, rows < N_PAD), 0.0, y).astype(BF16)

    @pl.when(j >= N_QKV_TILES)
    def _():
        o_ref[...] = jax.nn.sigmoid(acc).astype(BF16)


def _in_proj(hb, w_in_t, layer):
    l, d = hb.shape
    n_tiles = N_QKV_TILES + 2 * D_MODEL // TN_PROJ
    return pl.pallas_call(
        functools.partial(_in_proj_kernel, layer),
        grid=(n_tiles, l // TM_PROJ),
        in_specs=[
            pl.BlockSpec((TM_PROJ, d), lambda j, i: (i, 0)),
            pl.BlockSpec(memory_space=pl.ANY),
        ],
        out_specs=pl.BlockSpec((TM_PROJ, TN_PROJ), lambda j, i: (i, j)),
        out_shape=jax.ShapeDtypeStruct((l, n_tiles * TN_PROJ), BF16),
        scratch_shapes=[pltpu.VMEM((2, TN_PROJ, d), F32), pltpu.VMEM((TN_PROJ, d), BF16),
                        pltpu.SemaphoreType.DMA((2,))],
        compiler_params=_cparams(("arbitrary", "arbitrary")),
        name="in_proj",
    )(hb, w_in_t)


def _forget_kernel(h_ref, w_ref, b_ref, o_ref, carry_ref):
    i = pl.program_id(0)

    @pl.when(i == 0)
    def _():
        carry_ref[...] = jnp.zeros_like(carry_ref)

    f = lax.dot_general(w_ref[...], h_ref[...], (((1,), (1,)), ((), ())),
                        precision=lax.Precision.HIGHEST, preferred_element_type=F32) + b_ref[...]
    pos = i * TM_LN + lax.broadcasted_iota(jnp.int32, f.shape, 1)
    lf = jnp.where(pos >= N_PAD, _log_sigmoid(f), 0.0)
    r = lax.broadcasted_iota(jnp.int32, (TM_LN, TM_LN), 0)
    c = lax.broadcasted_iota(jnp.int32, (TM_LN, TM_LN), 1)
    upto = jnp.where(r <= c, 1.0, 0.0).astype(F32)
    cs = jnp.dot(lf, upto, precision=lax.Precision.HIGHEST, preferred_element_type=F32) + carry_ref[:, 0:1]
    o_ref[...] = jnp.where(pos >= N_PAD, -cs, NEG)
    carry_ref[...] = jnp.broadcast_to(cs[:, TM_LN - 1:TM_LN], carry_ref.shape)


def _forget_bias(hf, w_in_t, bf, layer):
    l, d = hf.shape
    return pl.pallas_call(
        _forget_kernel,
        grid=(l // TM_LN,),
        in_specs=[
            pl.BlockSpec((TM_LN, d), lambda i: (i, 0)),
            pl.BlockSpec((None, N_HEADS, d), lambda i: (layer, F_COL0 // N_HEADS, 0)),
            pl.BlockSpec((N_HEADS, TM_LN), lambda i: (0, 0)),
        ],
        out_specs=pl.BlockSpec((N_HEADS, TM_LN), lambda i: (0, i)),
        out_shape=jax.ShapeDtypeStruct((N_HEADS, l), F32),
        scratch_shapes=[pltpu.VMEM((N_HEADS, LANE), F32)],
        compiler_params=_cparams(("arbitrary",)),
        name="forget_bias",
    )(hf, w_in_t, bf)


def _sb_kernel(q_ref, k_ref, v_ref, o_ref, run_ref, acc_ref):
    i = pl.program_id(0)
    tq, tk = SB_TQ, SB_TK
    r = lax.broadcasted_iota(jnp.int32, (2 * tk, 2 * tk), 0)
    c = lax.broadcasted_iota(jnp.int32, (2 * tk, 2 * tk), 1)
    suffix = jnp.where(jnp.logical_or(c >= tk, (r % tk) > c), 1.0, 0.0).astype(BF16)
    row = lax.broadcasted_iota(jnp.int32, (tq, tk), 0)
    col = lax.broadcasted_iota(jnp.int32, (tq, tk), 1)
    strictly_causal = col < row

    def block(j, diagonal):
        start = pl.multiple_of(j * tk, tk)
        heads = range(N_HEADS)
        cols = [slice(hd * HEAD_DIM, (hd + 1) * HEAD_DIM) for hd in heads]
        zs = [lax.dot_general(q_ref[:, c], k_ref[pl.ds(start, tk), c], (((1,), (1,)), ((), ())),
                              preferred_element_type=F32) for c in cols]
        log_betas, cats = [], []
        for z in zs:
            log_beta = _log_sigmoid(z)
            log_keep = log_beta - z
            if diagonal:
                log_keep = jnp.where(strictly_causal, log_keep, 0.0)
            hi = log_keep.astype(BF16)
            lo = (log_keep - hi.astype(F32)).astype(BF16)
            log_betas.append(log_beta)
            cats.append(jnp.concatenate([hi, lo], axis=1))
        sums = [jnp.dot(cat, suffix, preferred_element_type=F32) for cat in cats]
        ws, worst = [], None
        for c, log_beta, s in zip(cols, log_betas, sums):
            if diagonal:
                w = jnp.where(strictly_causal, jnp.exp(log_beta + s[:, :tk]), 0.0)
                run = s[:, tk:]
            else:
                run = run_ref[:, c]
                w = jnp.exp(log_beta + s[:, :tk] + run)
                run = run + s[:, tk:]
            run_ref[:, c] = run
            ws.append(w.astype(BF16))
            worst = run if worst is None else jnp.maximum(worst, run)
        for c, w in zip(cols, ws):
            pv = jnp.dot(w, v_ref[pl.ds(start, tk), c], preferred_element_type=F32)
            if diagonal:
                acc_ref[:, c] = pv
            else:
                acc_ref[:, c] += pv
        return jnp.max(worst)

    worst = block(i, True)

    def cond(s):
        j, worst = s
        return jnp.logical_and(j >= 0, worst > SB_EXIT)

    def body(s):
        j, _ = s
        return j - 1, block(j, False)

    lax.while_loop(cond, body, (i - 1, worst))
    o_ref[...] = acc_ref[...].astype(BF16)


def _sb_attention(proj):
    l = proj.shape[0]
    resident = lambda blk: pl.BlockSpec((l, WIDTH), lambda i: (0, blk), pipeline_mode=pl.Buffered(1))
    return pl.pallas_call(
        _sb_kernel,
        grid=(l // SB_TQ,),
        in_specs=[pl.BlockSpec((SB_TQ, WIDTH), lambda i: (i, 0)), resident(1), resident(2)],
        out_specs=pl.BlockSpec((SB_TQ, WIDTH), lambda i: (i, 0)),
        out_shape=jax.ShapeDtypeStruct((l, WIDTH), BF16),
        scratch_shapes=[pltpu.VMEM((SB_TQ, WIDTH), F32), pltpu.VMEM((SB_TQ, WIDTH), F32)],
        compiler_params=_cparams(("arbitrary",)),
        name="sb_attention",
    )(proj, proj, proj)


def _fox_kernel(q_ref, k_ref, v_ref, b_ref, o_ref, vaug_ref):
    i = pl.program_id(1)
    t = FOX_T

    @pl.when(i == 0)
    def _():
        lane = lax.broadcasted_iota(jnp.int32, (vaug_ref.shape[1], HEAD_DIM), 1)
        ones_col = jnp.where(lane == 0, 1.0, 0.0).astype(BF16)
        for hd in range(FOX_GROUP):
            vaug_ref[hd, :, :HEAD_DIM] = v_ref[:, hd * HEAD_DIM:(hd + 1) * HEAD_DIM]
            vaug_ref[hd, :, HEAD_DIM:] = ones_col

    row = lax.broadcasted_iota(jnp.int32, (t, t), 0)
    col = lax.broadcasted_iota(jnp.int32, (t, t), 1)
    causal = col <= row

    def block(j, state, diagonal):
        start = pl.multiple_of(j * t, t)
        heads = range(FOX_GROUP)
        cols = [slice(hd * HEAD_DIM, (hd + 1) * HEAD_DIM) for hd in heads]
        scores = [lax.dot_general(q_ref[:, c], k_ref[pl.ds(start, t), c], (((1,), (1,)), ((), ())),
                                  preferred_element_type=F32) for c in cols]
        ps, scales, ms = [], [], []
        for hd, s in zip(heads, scores):
            m = state[hd][0]
            s = s + b_ref[hd, j]
            if diagonal:
                s = jnp.where(causal, s, NEG)
            m_new = jnp.maximum(m, jnp.max(s, axis=1, keepdims=True))
            ps.append(jnp.exp(s - m_new).astype(BF16))
            scales.append(jnp.exp(m - m_new))
            ms.append(m_new)
        out = []
        for hd in heads:
            pv = jnp.dot(ps[hd], vaug_ref[hd, pl.ds(start, t), :], preferred_element_type=F32)
            out.append((ms[hd], scales[hd] * state[hd][1] + pv))
        return tuple(out)

    init = tuple((jnp.full((t, 1), NEG, F32), jnp.zeros((t, 2 * HEAD_DIM), F32))
                 for _ in range(FOX_GROUP))
    state = block(i, init, True)
    state = lax.fori_loop(0, i, lambda j, s: block(j, s, False), state)
    for hd, (_, acc) in enumerate(state):
        o_ref[:, hd * HEAD_DIM:(hd + 1) * HEAD_DIM] = (
            acc[:, :HEAD_DIM] / acc[:, HEAD_DIM:HEAD_DIM + 1]).astype(BF16)


def _fox_attention(proj, key_bias):
    l = proj.shape[0]
    gw = FOX_GROUP * HEAD_DIM
    first = 3 * WIDTH // gw
    per = WIDTH // gw
    nkb = l // FOX_T
    return pl.pallas_call(
        _fox_kernel,
        grid=(N_HEADS // FOX_GROUP, l // FOX_T),
        in_specs=[
            pl.BlockSpec((FOX_T, gw), lambda g, i: (i, first + g)),
            pl.BlockSpec((l, gw), lambda g, i: (0, first + per + g), pipeline_mode=pl.Buffered(1)),
            pl.BlockSpec((l, gw), lambda g, i: (0, first + 2 * per + g), pipeline_mode=pl.Buffered(1)),
            pl.BlockSpec((FOX_GROUP, nkb, 1, FOX_T), lambda g, i: (g, 0, 0, 0)),
        ],
        out_specs=pl.BlockSpec((FOX_T, gw), lambda g, i: (i, g)),
        out_shape=jax.ShapeDtypeStruct((l, WIDTH), BF16),
        scratch_shapes=[pltpu.VMEM((FOX_GROUP, l, 2 * HEAD_DIM), BF16)],
        compiler_params=_cparams(("arbitrary", "arbitrary")),
        name="fox_attention",
    )(proj, proj, proj, key_bias)


def _merge_kernel(ys_ref, yf_ref, gs_ref, gf_ref, ws_ref, wf_ref, o_ref, wsb_ref, wfb_ref):
    i = pl.program_id(1)

    @pl.when(i == 0)
    def _():
        wsb_ref[...] = ws_ref[...].astype(BF16)
        wfb_ref[...] = wf_ref[...].astype(BF16)

    a = jnp.dot(ys_ref[...], wsb_ref[...], preferred_element_type=F32)
    b = jnp.dot(yf_ref[...], wfb_ref[...], preferred_element_type=F32)
    o_ref[...] = (gs_ref[...].astype(F32) * a + gf_ref[...].astype(F32) * b).astype(BF16)


def _merge(y_sb, y_fx, proj, w_bsb, w_bfx, layer):
    l = y_sb.shape[0]
    d = D_MODEL
    nj = d // TN_PROJ
    act = pl.BlockSpec((TM_PROJ, WIDTH), lambda j, i: (i, 0))
    wspec = pl.BlockSpec((None, WIDTH, TN_PROJ), lambda j, i: (layer, 0, j))
    return pl.pallas_call(
        _merge_kernel,
        grid=(nj, l // TM_PROJ),
        in_specs=[
            act, act,
            pl.BlockSpec((TM_PROJ, TN_PROJ), lambda j, i: (i, N_QKV_TILES + j)),
            pl.BlockSpec((TM_PROJ, TN_PROJ), lambda j, i: (i, N_QKV_TILES + nj + j)),
            wspec, wspec,
        ],
        out_specs=pl.BlockSpec((TM_PROJ, TN_PROJ), lambda j, i: (i, j)),
        out_shape=jax.ShapeDtypeStruct((l, d), BF16),
        scratch_shapes=[pltpu.VMEM((WIDTH, TN_PROJ), BF16), pltpu.VMEM((WIDTH, TN_PROJ), BF16)],
        compiler_params=_cparams(("arbitrary", "arbitrary")),
        name="merge",
    )(y_sb, y_fx, proj, proj, w_bsb, w_bfx)


def _pick(index, values):
    out = values[0]
    for k in range(1, len(values)):
        out = jnp.where(index == k, values[k], out)
    return out


def _grouped_top2(sel, aff):
    srow = [sel[e:e + 1, :] for e in range(N_EXPERTS)]
    arow = [aff[e:e + 1, :] for e in range(N_EXPERTS)]
    n = EXPERTS_PER_GROUP
    score = []
    for g in range(N_GROUPS):
        v = srow[g * n:(g + 1) * n]
        pair = [v[a] + v[b] for a in range(n) for b in range(a + 1, n)]
        score.append(functools.reduce(jnp.maximum, pair))
    group = jnp.zeros_like(score[0], dtype=jnp.int32)
    best = score[0]
    for g in range(1, N_GROUPS):
        better = score[g] > best
        group = jnp.where(better, g, group)
        best = jnp.where(better, score[g], best)
    v = [_pick(group, [srow[g * n + k] for g in range(N_GROUPS)]) for k in range(n)]
    a = [_pick(group, [arow[g * n + k] for g in range(N_GROUPS)]) for k in range(n)]
    i1 = jnp.zeros_like(group)
    m1 = v[0]
    for k in range(1, n):
        better = v[k] > m1
        i1 = jnp.where(better, k, i1)
        m1 = jnp.where(better, v[k], m1)
    first_is_0 = i1 == 0
    i2 = jnp.where(first_is_0, 1, 0)
    m2 = jnp.where(first_is_0, v[1], v[0])
    for k in range(1, n):
        better = jnp.logical_and(i1 != k, v[k] > m2)
        i2 = jnp.where(better, k, i2)
        m2 = jnp.where(better, v[k], m2)
    a1 = _pick(i1, a)
    a2 = _pick(i2, a)
    tot = a1 + a2
    return group * n + i1, group * n + i2, a1 / tot, a2 / tot


def _out_ln_kernel(x_ref, w_ref, h_ref, g_ref, b_ref, wr_ref, rb_ref,
                   of_ref, ob_ref, hp_ref, ri_ref, rw_ref, cnt_ref, acc_ref):
    i = pl.program_id(0)
    k = pl.program_id(1)

    @pl.when(k == 0)
    def _():
        acc_ref[...] = jnp.zeros_like(acc_ref)

    @pl.when(jnp.logical_and(i == 0, k == 0))
    def _():
        cnt_ref[...] = jnp.zeros_like(cnt_ref)

    acc_ref[...] += jnp.dot(x_ref[...], w_ref[...].astype(BF16), preferred_element_type=F32)

    @pl.when(k == pl.num_programs(1) - 1)
    def _():
        y = _layer_norm(ALPHA * h_ref[...] + acc_ref[...], g_ref[...], b_ref[...])
        of_ref[...] = y
        ob_ref[...] = y.astype(BF16)
        hp_ref[...] = _pack_bf16_pairs(y)

        logits = lax.dot_general(wr_ref[...], y, (((1,), (1,)), ((), ())),
                                 precision=lax.Precision.HIGHEST, preferred_element_type=F32)
        aff = jax.nn.sigmoid(logits)
        e1, e2, w1, w2 = _grouped_top2(aff + rb_ref[...], aff)
        t = TM_LN
        eid = lax.broadcasted_iota(jnp.int32, (N_EXPERTS, t), 0)
        hit1 = eid == e1
        hit2 = eid == e2
        onehot = jnp.where(jnp.logical_or(hit1, hit2), 1.0, 0.0)
        r = lax.broadcasted_iota(jnp.int32, (t, t), 0)
        c = lax.broadcasted_iota(jnp.int32, (t, t), 1)
        before = jnp.where(r < c, 1.0, 0.0).astype(BF16)
        seen = jnp.dot(onehot.astype(BF16), before, preferred_element_type=F32) + cnt_ref[:, 0:1]
        rank1 = jnp.sum(jnp.where(hit1, seen, 0.0), axis=0, keepdims=True)
        rank2 = jnp.sum(jnp.where(hit2, seen, 0.0), axis=0, keepdims=True)
        ri_ref[0:1, :] = e1
        ri_ref[1:2, :] = e2
        ri_ref[2:3, :] = rank1.astype(jnp.int32)
        ri_ref[3:4, :] = rank2.astype(jnp.int32)
        ri_ref[4:8, :] = jnp.zeros((4, t), jnp.int32)
        rw_ref[0:1, :] = w1
        rw_ref[1:2, :] = w2
        rw_ref[2:8, :] = jnp.zeros((6, t), F32)
        cnt_ref[...] += jnp.sum(onehot, axis=1, keepdims=True)


def _out_ln(merged, w_out, hf, g, b, wr_t, rb, layer):
    l, d = hf.shape
    row = pl.BlockSpec((TM_LN, d), lambda i, k: (i, 0))
    vec = pl.BlockSpec((None, 1, d), lambda i, k: (layer, 0, 0))
    tok = pl.BlockSpec((8, TM_LN), lambda i, k: (0, i))
    return pl.pallas_call(
        _out_ln_kernel,
        grid=(l // TM_LN, d // TK_OUT),
        in_specs=[
            pl.BlockSpec((TM_LN, TK_OUT), lambda i, k: (i, k)),
            pl.BlockSpec((None, TK_OUT, d), lambda i, k: (layer, k, 0)),
            row, vec, vec,
            pl.BlockSpec((N_EXPERTS, d), lambda i, k: (0, 0)),
            pl.BlockSpec((N_EXPERTS, TM_LN), lambda i, k: (0, 0)),
        ],
        out_specs=[row, row, pl.BlockSpec((TM_LN, d // 2), lambda i, k: (i, 0)), tok, tok,
                   pl.BlockSpec((N_EXPERTS, LANE), lambda i, k: (0, 0))],
        out_shape=[jax.ShapeDtypeStruct((l, d), F32), jax.ShapeDtypeStruct((l, d), BF16),
                   jax.ShapeDtypeStruct((l, d // 2), jnp.uint32),
                   jax.ShapeDtypeStruct((8, l), jnp.int32), jax.ShapeDtypeStruct((8, l), F32),
                   jax.ShapeDtypeStruct((N_EXPERTS, LANE), F32)],
        scratch_shapes=[pltpu.VMEM((TM_LN, d), F32)],
        compiler_params=_cparams(("arbitrary", "arbitrary")),
        name="out_ln",
    )(merged, w_out, hf, g.reshape(DEPTH, 1, d), b.reshape(DEPTH, 1, d), wr_t, rb)


def _invert_kernel(pos1_ref, pos2_ref, tok_ref):
    def clear(r, carry):
        tok_ref[r] = 0
        return carry

    lax.fori_loop(0, tok_ref.shape[0], clear, 0, unroll=8)

    def mark(t, carry):
        tok_ref[pos1_ref[t]] = t
        tok_ref[pos2_ref[t]] = t
        return carry

    lax.fori_loop(0, pos1_ref.shape[0], mark, 0, unroll=8)


def _invert(pos1, pos2, n_rows):
    smem = pl.BlockSpec(memory_space=pltpu.SMEM)
    return pl.pallas_call(
        _invert_kernel,
        in_specs=[smem, smem],
        out_specs=smem,
        out_shape=jax.ShapeDtypeStruct((n_rows,), jnp.int32),
        name="invert",
    )(pos1, pos2)


def _pack_bf16_pairs(y):
    bits = pltpu.bitcast(y.astype(BF16).astype(F32), jnp.uint32)
    half = y.shape[1] // 2
    return (bits[:, :half] >> 16) | (bits[:, half:] & jnp.uint32(0xFFFF0000))


def _unpack_bf16_pairs(words):
    return (pltpu.bitcast(words << 16, F32), pltpu.bitcast(words & jnp.uint32(0xFFFF0000), F32))


def _expert_kernel(be_ref, nu_ref, nv_ref, tok_ref, hp_hbm, wg_ref, wu_ref, wd_ref, o_ref,
                   xs_ref, xb_ref, acc_ref, wgb_ref, wub_ref, wdb_ref, sem):
    blk = pl.program_id(0)
    f = pl.program_id(1)
    nf = EXPERT_FF // MOE_TF
    n_used = nu_ref[0]
    used = blk < n_used
    slot = blk % 2
    second = nv_ref[blk] > MOE_HALF
    next_second = nv_ref[blk + 1] > MOE_HALF
    per_step = MOE_HALF // nf
    halves = (slice(0, MOE_HALF), slice(MOE_HALF, MOE_TM))

    def row_copy(s, r, src_row):
        return pltpu.make_async_copy(hp_hbm.at[pl.ds(src_row, 1)], xs_ref.at[s, pl.ds(r, 1)], sem.at[s])

    def issue_loop(first_row):
        def issue(r, carry):
            row_copy(0, first_row + r, tok_ref[first_row + r]).start()
            return carry

        lax.fori_loop(0, MOE_HALF, issue, 0, unroll=8)

    def wait_half():
        def wait(r, carry):
            row_copy(slot, 0, 0).wait()
            return carry

        lax.fori_loop(0, MOE_HALF, wait, 0, unroll=8)

    def unpack(rows):
        lo, hi = _unpack_bf16_pairs(xs_ref[slot, rows, :])
        xb_ref[rows, :D_MODEL // 2] = lo.astype(BF16)
        xb_ref[rows, D_MODEL // 2:] = hi.astype(BF16)

    def compute(rows):
        x = xb_ref[rows, :]
        g = jnp.dot(x, wgb_ref[...], preferred_element_type=F32)
        u = jnp.dot(x, wub_ref[...], preferred_element_type=F32)
        a = (g * jax.nn.sigmoid(g) * u).astype(BF16)
        acc_ref[rows, :] += jnp.dot(a, wdb_ref[...], preferred_element_type=F32)

    @pl.when(jnp.logical_and(blk == 0, f == 0))
    def _():
        issue_loop(0)

        @pl.when(second)
        def _():
            issue_loop(MOE_HALF)

    @pl.when(jnp.logical_and(blk <= n_used, f == 0))
    def _():
        wait_half()

        @pl.when(second)
        def _():
            wait_half()

    @pl.when(jnp.logical_and(used, f == 0))
    def _():
        acc_ref[...] = jnp.zeros_like(acc_ref)
        unpack(halves[0])

        @pl.when(second)
        def _():
            unpack(halves[1])

    @pl.when(used)
    def _():
        nxt = (blk + 1) * MOE_TM
        for r in range(per_step):
            row = f * per_step + r
            row_copy(1 - slot, row, tok_ref[nxt + row]).start()

        @pl.when(next_second)
        def _():
            for r in range(per_step):
                row = MOE_HALF + f * per_step + r
                row_copy(1 - slot, row, tok_ref[nxt + row]).start()

        wgb_ref[...] = wg_ref[...].astype(BF16)
        wub_ref[...] = wu_ref[...].astype(BF16)
        wdb_ref[...] = wd_ref[...].astype(BF16)
        compute(halves[0])

        @pl.when(second)
        def _():
            compute(halves[1])

    @pl.when(f == nf - 1)
    def _():
        o_ref[...] = _pack_bf16_pairs(jnp.where(used, acc_ref[...], 0.0))


def _expert_ffn(block_e, n_used, n_valid, row_tok, hp, w_gate, w_up, w_down, layer):
    d = D_MODEL
    nb = block_e.shape[0]
    nf = EXPERT_FF // MOE_TF

    def w_in_map(b, f, be, nu, nv, tok):
        return (layer, be[b], 0, jnp.where(b < nu[0], f, nf - 1))

    def w_out_map(b, f, be, nu, nv, tok):
        return (layer, be[b], jnp.where(b < nu[0], f, nf - 1), 0)

    grid_spec = pltpu.PrefetchScalarGridSpec(
        num_scalar_prefetch=4,
        grid=(nb, nf),
        in_specs=[
            pl.BlockSpec(memory_space=pl.ANY),
            pl.BlockSpec((None, None, d, MOE_TF), w_in_map),
            pl.BlockSpec((None, None, d, MOE_TF), w_in_map),
            pl.BlockSpec((None, None, MOE_TF, d), w_out_map),
        ],
        out_specs=pl.BlockSpec((MOE_TM, d // 2), lambda b, f, be, nu, nv, tok: (b, 0)),
        scratch_shapes=[pltpu.VMEM((2, MOE_TM, d // 2), jnp.uint32), pltpu.VMEM((MOE_TM, d), BF16),
                        pltpu.VMEM((MOE_TM, d), F32), pltpu.VMEM((d, MOE_TF), BF16),
                        pltpu.VMEM((d, MOE_TF), BF16), pltpu.VMEM((MOE_TF, d), BF16),
                        pltpu.SemaphoreType.DMA((2,))],
    )
    return pl.pallas_call(
        _expert_kernel,
        grid_spec=grid_spec,
        out_shape=jax.ShapeDtypeStruct((nb * MOE_TM, d // 2), jnp.uint32),
        compiler_params=_cparams(("arbitrary", "arbitrary")),
        name="expert_ffn",
    )(block_e, n_used, n_valid, row_tok, hp, w_gate, w_up, w_down)


def _combine_kernel(first_row, with_bf16, pos1_ref, pos2_ref, yb_hbm, h_ref, w_ref, g_ref, b_ref, *rest):
    out_refs, (y1_ref, y2_ref, sem) = rest[:-3], rest[-3:]
    i = pl.program_id(0)
    slot = i % 2

    def row_copy(dst, s, r, src_row):
        return pltpu.make_async_copy(yb_hbm.at[pl.ds(src_row, 1)], dst.at[s, pl.ds(r, 1)], sem.at[s])

    def issue_tile(tile, s):
        base = first_row + tile * CMB_TM

        def issue(r, carry):
            row_copy(y1_ref, s, r, pos1_ref[base + r]).start()
            row_copy(y2_ref, s, r, pos2_ref[base + r]).start()
            return carry

        lax.fori_loop(0, CMB_TM, issue, 0, unroll=8)

    @pl.when(i == 0)
    def _():
        issue_tile(0, 0)

    @pl.when(i + 1 < pl.num_programs(0))
    def _():
        issue_tile(i + 1, 1 - slot)

    def wait(r, carry):
        row_copy(y1_ref, slot, 0, 0).wait()
        row_copy(y2_ref, slot, 0, 0).wait()
        return carry

    lax.fori_loop(0, CMB_TM, wait, 0, unroll=8)
    w = w_ref[...]
    lo1, hi1 = _unpack_bf16_pairs(y1_ref[slot])
    lo2, hi2 = _unpack_bf16_pairs(y2_ref[slot])
    y = jnp.concatenate([w[:, 0:1] * lo1 + w[:, 1:2] * lo2, w[:, 0:1] * hi1 + w[:, 1:2] * hi2], axis=1)
    out = _layer_norm(ALPHA * h_ref[...] + y, g_ref[...], b_ref[...])
    out_refs[0][...] = out
    if with_bf16:
        out_refs[1][...] = out.astype(BF16)


def _combine_ln(pos1, pos2, yb, hf, wts, g, b, layer, final):
    l, d = hf.shape
    first_block = PREFIX // CMB_TM if final else 0
    n_rows = l - first_block * CMB_TM
    src = lambda i, p1, p2: (i + first_block, 0)
    dst = lambda i, p1, p2: (i, 0)
    vec = pl.BlockSpec((None, 1, d), lambda i, p1, p2: (layer, 0, 0))
    out_specs = [pl.BlockSpec((CMB_TM, d), dst)]
    out_shape = [jax.ShapeDtypeStruct((n_rows, d), F32)]
    if not final:
        out_specs.append(pl.BlockSpec((CMB_TM, d), dst))
        out_shape.append(jax.ShapeDtypeStruct((n_rows, d), BF16))
    grid_spec = pltpu.PrefetchScalarGridSpec(
        num_scalar_prefetch=2,
        grid=(n_rows // CMB_TM,),
        in_specs=[
            pl.BlockSpec(memory_space=pl.ANY),
            pl.BlockSpec((CMB_TM, d), src),
            pl.BlockSpec((CMB_TM, TOP_K), src),
            vec, vec,
        ],
        out_specs=out_specs,
        scratch_shapes=[
            pltpu.VMEM((2, CMB_TM, d // 2), jnp.uint32),
            pltpu.VMEM((2, CMB_TM, d // 2), jnp.uint32),
            pltpu.SemaphoreType.DMA((2,)),
        ],
    )
    return pl.pallas_call(
        functools.partial(_combine_kernel, first_block * CMB_TM, not final),
        grid_spec=grid_spec,
        out_shape=out_shape,
        compiler_params=_cparams(("arbitrary",)),
        name="combine_ln",
    )(pos1, pos2, yb, hf, wts, g.reshape(DEPTH, 1, d), b.reshape(DEPTH, 1, d))


def _plan(route_i, counts_f):
    n = route_i.shape[1]
    nb = -(-(n * TOP_K + N_EXPERTS * (MOE_TM - 1)) // MOE_TM)
    counts = counts_f[:, 0].astype(jnp.int32)
    blocks_e = (counts + MOE_TM - 1) // MOE_TM
    bend = jnp.cumsum(blocks_e)
    base = (bend - blocks_e) * MOE_TM
    eids = jnp.arange(N_EXPERTS, dtype=jnp.int32)[:, None]
    base_of = lambda e: jnp.sum(jnp.where(e[None, :] == eids, base[:, None], 0), axis=0)
    pos1 = (route_i[2] + base_of(route_i[0])).astype(jnp.int32)
    pos2 = (route_i[3] + base_of(route_i[1])).astype(jnp.int32)
    n_used = bend[-1:].astype(jnp.int32)
    blocks = jnp.arange(nb + 1, dtype=jnp.int32)
    block_e = jnp.minimum(jnp.searchsorted(bend, blocks, side='right'), N_EXPERTS - 1).astype(jnp.int32)
    done = (blocks - (bend - blocks_e)[block_e]) * MOE_TM
    n_valid = jnp.where(blocks < n_used[0], jnp.clip(counts[block_e] - done, 0, MOE_TM), 0)
    n_valid = jnp.concatenate([n_valid, jnp.zeros((1,), jnp.int32)]).astype(jnp.int32)
    return pos1, pos2, block_e, n_used, n_valid


def kernel(x, meta_tokens, ln_in_g, ln_in_b, w_in, b_forget, w_branch_sb, w_branch_fox, w_out,
           ln_mix_g, ln_mix_b, w_router, router_bias, w_gate, w_up, w_down, ln_ffn_g, ln_ffn_b):
    b, s, d = x.shape
    assert b == 1 and d == D_MODEL
    l = s + PREFIX
    prefix = jnp.concatenate([jnp.zeros((N_PAD, d), x.dtype), meta_tokens.astype(x.dtype)], axis=0)
    hf, hb = _ln_in(prefix, x[0], ln_in_g, ln_in_b)

    w_in_t = jnp.swapaxes(w_in, 1, 2)
    wr_t = w_router.astype(F32).T
    rb = jnp.broadcast_to(router_bias.astype(F32)[:, None], (N_EXPERTS, TM_LN))
    for layer in range(DEPTH):
        proj = _in_proj(hb, w_in_t, layer)
        bf = jnp.broadcast_to(b_forget[layer].astype(F32)[:, None], (N_HEADS, TM_LN))
        key_bias = _forget_bias(hf, w_in_t, bf, layer).reshape(N_HEADS, l // FOX_T, 1, FOX_T)
        y_sb = _sb_attention(proj)
        y_fx = _fox_attention(proj, key_bias)
        merged = _merge(y_sb, y_fx, proj, w_branch_sb, w_branch_fox, layer)
        hf, hb, hp, route_i, route_w, counts = _out_ln(merged, w_out, hf, ln_mix_g, ln_mix_b,
                                                       wr_t, rb, layer)

        pos1, pos2, block_e, n_used, n_valid = _plan(route_i, counts)
        row_tok = _invert(pos1, pos2, block_e.shape[0] * MOE_TM)
        yb = _expert_ffn(block_e, n_used, n_valid, row_tok, hp, w_gate, w_up, w_down, layer)
        final = layer == DEPTH - 1
        outs = _combine_ln(pos1, pos2, yb, hf, route_w[:TOP_K].T, ln_ffn_g, ln_ffn_b, layer, final)
        if not final:
            hf, hb = outs
    return outs[0][None]
```

```python
import functools

import jax
import jax.numpy as jnp
from jax import lax
from jax.experimental import pallas as pl
from jax.experimental.pallas import tpu as pltpu

F32 = jnp.float32
BF16 = jnp.bfloat16

D_MODEL = 2048
DEPTH = 2
HEAD_DIM = 128
N_HEADS = 8
WIDTH = N_HEADS * HEAD_DIM
N_META = 16
N_PAD = 112
PREFIX = N_PAD + N_META
N_EXPERTS = 16
N_GROUPS = 4
EXPERTS_PER_GROUP = N_EXPERTS // N_GROUPS
TOP_K = 2
EXPERT_FF = 1024
ALPHA = (2 * DEPTH) ** 0.25
LN_EPS = 1e-5
NEG = -1e30
QK_SCALE = HEAD_DIM ** -0.5
QKV_COLS = 6 * WIDTH
F_COL0 = QKV_COLS
G_COL0 = QKV_COLS + N_HEADS

SB_EXIT = -104.0

LANE = 128
VMEM_LIMIT = 56 * 1024 * 1024
TM_PROJ = 1664
TN_PROJ = 512
N_QKV_TILES = QKV_COLS // TN_PROJ
TM_LN = 640
W_CHUNK = 256
SB_TQ = 128
SB_TK = 128
FOX_T = 640
FOX_GROUP = 4
MOE_CHUNK = 384
MOE_CHUNKS = 3
MOE_TM = MOE_CHUNK * MOE_CHUNKS
MOE_TF = 256
CMB_TM = 128


def _cparams(sem):
    return pltpu.CompilerParams(dimension_semantics=sem, vmem_limit_bytes=VMEM_LIMIT)


def _layer_norm(x, g, b):
    mu = jnp.mean(x, axis=-1, keepdims=True)
    xc = x - mu
    var = jnp.mean(xc * xc, axis=-1, keepdims=True)
    return xc * lax.rsqrt(var + LN_EPS) * g + b


def _log_sigmoid(z):
    return jnp.minimum(z, 0.0) - jnp.log(1.0 + jnp.exp(-jnp.abs(z)))


def _ln_in_kernel(p_ref, x_ref, g_ref, b_ref, of_ref, ob_ref):
    src = jnp.where(pl.program_id(0) == 0, p_ref[...], x_ref[...])
    y = _layer_norm(src, g_ref[...], b_ref[...])
    of_ref[...] = y
    ob_ref[...] = y.astype(BF16)


def _ln_in(prefix, x, g, b):
    s, d = x.shape
    l = s + PREFIX
    row = pl.BlockSpec((PREFIX, d), lambda i: (i, 0))
    vec = pl.BlockSpec((1, d), lambda i: (0, 0))
    return pl.pallas_call(
        _ln_in_kernel,
        grid=(l // PREFIX,),
        in_specs=[
            pl.BlockSpec((PREFIX, d), lambda i: (0, 0)),
            pl.BlockSpec((PREFIX, d), lambda i: (jnp.maximum(i - 1, 0), 0)),
            vec, vec,
        ],
        out_specs=[row, row],
        out_shape=[jax.ShapeDtypeStruct((l, d), F32), jax.ShapeDtypeStruct((l, d), BF16)],
        compiler_params=_cparams(("arbitrary",)),
        name="ln_in",
    )(prefix, x, g.reshape(1, d), b.reshape(1, d))


def _in_proj_kernel(layer, x_ref, wt_hbm, o_ref, stage_ref, wb_ref, sem):
    j = pl.program_id(0)
    i = pl.program_id(1)

    def w_copy(tile, slot):
        row0 = jnp.where(tile < N_QKV_TILES, tile * TN_PROJ, G_COL0 + (tile - N_QKV_TILES) * TN_PROJ)
        row0 = pl.multiple_of(row0, 8)
        return pltpu.make_async_copy(wt_hbm.at[layer, pl.ds(row0, TN_PROJ), :],
                                     stage_ref.at[slot], sem.at[slot])

    @pl.when(i == 0)
    def _():
        @pl.when(j == 0)
        def _():
            w_copy(0, 0).start()

        @pl.when(j + 1 < pl.num_programs(0))
        def _():
            w_copy(j + 1, (j + 1) % 2).start()

        w_copy(j, j % 2).wait()
        wb_ref[...] = stage_ref[j % 2].astype(BF16)

    acc = lax.dot_general(x_ref[...], wb_ref[...], (((1,), (1,)), ((), ())),
                          preferred_element_type=F32)

    @pl.when(j < N_QKV_TILES)
    def _():
        part = j % (3 * WIDTH // TN_PROJ)
        is_q = part < WIDTH // TN_PROJ
        is_v = part >= 2 * WIDTH // TN_PROJ
        y = acc * jnp.where(is_q, QK_SCALE, 1.0)
        rows = i * TM_PROJ + lax.broadcasted_iota(jnp.int32, acc.shape, 0)
        o_ref[...] = jnp.where(jnp.logical_and(is_v, rows < N_PAD), 0.0, y).astype(BF16)

    @pl.when(j >= N_QKV_TILES)
    def _():
        o_ref[...] = jax.nn.sigmoid(acc).astype(BF16)


def _in_proj(hb, w_in_t, layer):
    l, d = hb.shape
    n_tiles = N_QKV_TILES + 2 * D_MODEL // TN_PROJ
    return pl.pallas_call(
        functools.partial(_in_proj_kernel, layer),
        grid=(n_tiles, l // TM_PROJ),
        in_specs=[
            pl.BlockSpec((TM_PROJ, d), lambda j, i: (i, 0)),
            pl.BlockSpec(memory_space=pl.ANY),
        ],
        out_specs=pl.BlockSpec((TM_PROJ, TN_PROJ), lambda j, i: (i, j)),
        out_shape=jax.ShapeDtypeStruct((l, n_tiles * TN_PROJ), BF16),
        scratch_shapes=[pltpu.VMEM((2, TN_PROJ, d), F32), pltpu.VMEM((TN_PROJ, d), BF16),
                        pltpu.SemaphoreType.DMA((2,))],
        compiler_params=_cparams(("arbitrary", "arbitrary")),
        name="in_proj",
    )(hb, w_in_t)


def _forget_kernel(h_ref, w_ref, b_ref, o_ref, carry_ref):
    i = pl.program_id(0)

    @pl.when(i == 0)
    def _():
        carry_ref[...] = jnp.zeros_like(carry_ref)

    f = lax.dot_general(w_ref[...], h_ref[...], (((1,), (1,)), ((), ())),
                        precision=lax.Precision.HIGHEST, preferred_element_type=F32) + b_ref[...]
    pos = i * TM_LN + lax.broadcasted_iota(jnp.int32, f.shape, 1)
    lf = jnp.where(pos >= N_PAD, _log_sigmoid(f), 0.0)
    r = lax.broadcasted_iota(jnp.int32, (TM_LN, TM_LN), 0)
    c = lax.broadcasted_iota(jnp.int32, (TM_LN, TM_LN), 1)
    upto = jnp.where(r <= c, 1.0, 0.0).astype(F32)
    cs = jnp.dot(lf, upto, precision=lax.Precision.HIGHEST, preferred_element_type=F32) + carry_ref[:, 0:1]
    o_ref[...] = jnp.where(pos >= N_PAD, -cs, NEG)
    carry_ref[...] = jnp.broadcast_to(cs[:, TM_LN - 1:TM_LN], carry_ref.shape)


def _forget_bias(hf, w_in_t, bf, layer):
    l, d = hf.shape
    return pl.pallas_call(
        _forget_kernel,
        grid=(l // TM_LN,),
        in_specs=[
            pl.BlockSpec((TM_LN, d), lambda i: (i, 0)),
            pl.BlockSpec((None, N_HEADS, d), lambda i: (layer, F_COL0 // N_HEADS, 0)),
            pl.BlockSpec((N_HEADS, TM_LN), lambda i: (0, 0)),
        ],
        out_specs=pl.BlockSpec((N_HEADS, TM_LN), lambda i: (0, i)),
        out_shape=jax.ShapeDtypeStruct((N_HEADS, l), F32),
        scratch_shapes=[pltpu.VMEM((N_HEADS, LANE), F32)],
        compiler_params=_cparams(("arbitrary",)),
        name="forget_bias",
    )(hf, w_in_t, bf)


def _sb_kernel(q_ref, k_ref, v_ref, o_ref, run_ref, acc_ref):
    i = pl.program_id(0)
    tq, tk = SB_TQ, SB_TK
    r = lax.broadcasted_iota(jnp.int32, (2 * tk, 2 * tk), 0)
    c = lax.broadcasted_iota(jnp.int32, (2 * tk, 2 * tk), 1)
    suffix = jnp.where(jnp.logical_or(c >= tk, (r % tk) > c), 1.0, 0.0).astype(BF16)
    row = lax.broadcasted_iota(jnp.int32, (tq, tk), 0)
    col = lax.broadcasted_iota(jnp.int32, (tq, tk), 1)
    strictly_causal = col < row

    def block(j, diagonal):
        start = pl.multiple_of(j * tk, tk)
        heads = range(N_HEADS)
        cols = [slice(hd * HEAD_DIM, (hd + 1) * HEAD_DIM) for hd in heads]
        zs = [lax.dot_general(q_ref[:, c], k_ref[pl.ds(start, tk), c], (((1,), (1,)), ((), ())),
                              preferred_element_type=F32) for c in cols]
        log_betas, cats = [], []
        for z in zs:
            log_beta = _log_sigmoid(z)
            log_keep = log_beta - z
            if diagonal:
                log_keep = jnp.where(strictly_causal, log_keep, 0.0)
            hi = log_keep.astype(BF16)
            lo = (log_keep - hi.astype(F32)).astype(BF16)
            log_betas.append(log_beta)
            cats.append(jnp.concatenate([hi, lo], axis=1))
        sums = [jnp.dot(cat, suffix, preferred_element_type=F32) for cat in cats]
        ws, worst = [], None
        for c, log_beta, s in zip(cols, log_betas, sums):
            if diagonal:
                w = jnp.where(strictly_causal, jnp.exp(log_beta + s[:, :tk]), 0.0)
                run = s[:, tk:]
            else:
                run = run_ref[:, c]
                w = jnp.exp(log_beta + s[:, :tk] + run)
                run = run + s[:, tk:]
            run_ref[:, c] = run
            ws.append(w.astype(BF16))
            worst = run if worst is None else jnp.maximum(worst, run)
        for c, w in zip(cols, ws):
            pv = jnp.dot(w, v_ref[pl.ds(start, tk), c], preferred_element_type=F32)
            if diagonal:
                acc_ref[:, c] = pv
            else:
                acc_ref[:, c] += pv
        return jnp.max(worst)

    worst = block(i, True)

    def cond(s):
        j, worst = s
        return jnp.logical_and(j >= 0, worst > SB_EXIT)

    def body(s):
        j, _ = s
        return j - 1, block(j, False)

    lax.while_loop(cond, body, (i - 1, worst))
    o_ref[...] = acc_ref[...].astype(BF16)


def _sb_attention(proj):
    l = proj.shape[0]
    resident = lambda blk: pl.BlockSpec((l, WIDTH), lambda i: (0, blk), pipeline_mode=pl.Buffered(1))
    return pl.pallas_call(
        _sb_kernel,
        grid=(l // SB_TQ,),
        in_specs=[pl.BlockSpec((SB_TQ, WIDTH), lambda i: (i, 0)), resident(1), resident(2)],
        out_specs=pl.BlockSpec((SB_TQ, WIDTH), lambda i: (i, 0)),
        out_shape=jax.ShapeDtypeStruct((l, WIDTH), BF16),
        scratch_shapes=[pltpu.VMEM((SB_TQ, WIDTH), F32), pltpu.VMEM((SB_TQ, WIDTH), F32)],
        compiler_params=_cparams(("arbitrary",)),
        name="sb_attention",
    )(proj, proj, proj)


def _fox_kernel(q_ref, k_ref, v_ref, b_ref, o_ref, vaug_ref):
    i = pl.program_id(1)
    t = FOX_T

    @pl.when(i == 0)
    def _():
        lane = lax.broadcasted_iota(jnp.int32, (vaug_ref.shape[1], HEAD_DIM), 1)
        ones_col = jnp.where(lane == 0, 1.0, 0.0).astype(BF16)
        for hd in range(FOX_GROUP):
            vaug_ref[hd, :, :HEAD_DIM] = v_ref[:, hd * HEAD_DIM:(hd + 1) * HEAD_DIM]
            vaug_ref[hd, :, HEAD_DIM:] = ones_col

    row = lax.broadcasted_iota(jnp.int32, (t, t), 0)
    col = lax.broadcasted_iota(jnp.int32, (t, t), 1)
    causal = col <= row

    def block(j, state, diagonal):
        start = pl.multiple_of(j * t, t)
        heads = range(FOX_GROUP)
        cols = [slice(hd * HEAD_DIM, (hd + 1) * HEAD_DIM) for hd in heads]
        scores = [lax.dot_general(q_ref[:, c], k_ref[pl.ds(start, t), c], (((1,), (1,)), ((), ())),
                                  preferred_element_type=F32) for c in cols]
        ps, scales, ms = [], [], []
        for hd, s in zip(heads, scores):
            m = state[hd][0]
            s = s + b_ref[hd, j]
            if diagonal:
                s = jnp.where(causal, s, NEG)
            m_new = jnp.maximum(m, jnp.max(s, axis=1, keepdims=True))
            ps.append(jnp.exp(s - m_new).astype(BF16))
            scales.append(jnp.exp(m - m_new))
            ms.append(m_new)
        out = []
        for hd in heads:
            pv = jnp.dot(ps[hd], vaug_ref[hd, pl.ds(start, t), :], preferred_element_type=F32)
            out.append((ms[hd], scales[hd] * state[hd][1] + pv))
        return tuple(out)

    init = tuple((jnp.full((t, 1), NEG, F32), jnp.zeros((t, 2 * HEAD_DIM), F32))
                 for _ in range(FOX_GROUP))
    state = block(i, init, True)
    state = lax.fori_loop(0, i, lambda j, s: block(j, s, False), state)
    for hd, (_, acc) in enumerate(state):
        o_ref[:, hd * HEAD_DIM:(hd + 1) * HEAD_DIM] = (
            acc[:, :HEAD_DIM] / acc[:, HEAD_DIM:HEAD_DIM + 1]).astype(BF16)


def _fox_attention(proj, key_bias):
    l = proj.shape[0]
    gw = FOX_GROUP * HEAD_DIM
    first = 3 * WIDTH // gw
    per = WIDTH // gw
    nkb = l // FOX_T
    return pl.pallas_call(
        _fox_kernel,
        grid=(N_HEADS // FOX_GROUP, l // FOX_T),
        in_specs=[
            pl.BlockSpec((FOX_T, gw), lambda g, i: (i, first + g)),
            pl.BlockSpec((l, gw), lambda g, i: (0, first + per + g), pipeline_mode=pl.Buffered(1)),
            pl.BlockSpec((l, gw), lambda g, i: (0, first + 2 * per + g), pipeline_mode=pl.Buffered(1)),
            pl.BlockSpec((FOX_GROUP, nkb, 1, FOX_T), lambda g, i: (g, 0, 0, 0)),
        ],
        out_specs=pl.BlockSpec((FOX_T, gw), lambda g, i: (i, g)),
        out_shape=jax.ShapeDtypeStruct((l, WIDTH), BF16),
        scratch_shapes=[pltpu.VMEM((FOX_GROUP, l, 2 * HEAD_DIM), BF16)],
        compiler_params=_cparams(("arbitrary", "arbitrary")),
        name="fox_attention",
    )(proj, proj, proj, key_bias)


def _merge_kernel(ys_ref, yf_ref, gs_ref, gf_ref, ws_ref, wf_ref, o_ref, wsb_ref, wfb_ref):
    i = pl.program_id(1)

    @pl.when(i == 0)
    def _():
        wsb_ref[...] = ws_ref[...].astype(BF16)
        wfb_ref[...] = wf_ref[...].astype(BF16)

    a = jnp.dot(ys_ref[...], wsb_ref[...], preferred_element_type=F32)
    b = jnp.dot(yf_ref[...], wfb_ref[...], preferred_element_type=F32)
    o_ref[...] = (gs_ref[...].astype(F32) * a + gf_ref[...].astype(F32) * b).astype(BF16)


def _merge(y_sb, y_fx, proj, w_bsb, w_bfx, layer):
    l = y_sb.shape[0]
    d = D_MODEL
    nj = d // TN_PROJ
    act = pl.BlockSpec((TM_PROJ, WIDTH), lambda j, i: (i, 0))
    wspec = pl.BlockSpec((None, WIDTH, TN_PROJ), lambda j, i: (layer, 0, j))
    return pl.pallas_call(
        _merge_kernel,
        grid=(nj, l // TM_PROJ),
        in_specs=[
            act, act,
            pl.BlockSpec((TM_PROJ, TN_PROJ), lambda j, i: (i, N_QKV_TILES + j)),
            pl.BlockSpec((TM_PROJ, TN_PROJ), lambda j, i: (i, N_QKV_TILES + nj + j)),
            wspec, wspec,
        ],
        out_specs=pl.BlockSpec((TM_PROJ, TN_PROJ), lambda j, i: (i, j)),
        out_shape=jax.ShapeDtypeStruct((l, d), BF16),
        scratch_shapes=[pltpu.VMEM((WIDTH, TN_PROJ), BF16), pltpu.VMEM((WIDTH, TN_PROJ), BF16)],
        compiler_params=_cparams(("arbitrary", "arbitrary")),
        name="merge",
    )(y_sb, y_fx, proj, proj, w_bsb, w_bfx)


def _pick(index, values):
    out = values[0]
    for k in range(1, len(values)):
        out = jnp.where(index == k, values[k], out)
    return out


def _grouped_top2(sel, aff):
    srow = [sel[e:e + 1, :] for e in range(N_EXPERTS)]
    arow = [aff[e:e + 1, :] for e in range(N_EXPERTS)]
    n = EXPERTS_PER_GROUP
    score = []
    for g in range(N_GROUPS):
        v = srow[g * n:(g + 1) * n]
        pair = [v[a] + v[b] for a in range(n) for b in range(a + 1, n)]
        score.append(functools.reduce(jnp.maximum, pair))
    group = jnp.zeros_like(score[0], dtype=jnp.int32)
    best = score[0]
    for g in range(1, N_GROUPS):
        better = score[g] > best
        group = jnp.where(better, g, group)
        best = jnp.where(better, score[g], best)
    v = [_pick(group, [srow[g * n + k] for g in range(N_GROUPS)]) for k in range(n)]
    a = [_pick(group, [arow[g * n + k] for g in range(N_GROUPS)]) for k in range(n)]
    i1 = jnp.zeros_like(group)
    m1 = v[0]
    for k in range(1, n):
        better = v[k] > m1
        i1 = jnp.where(better, k, i1)
        m1 = jnp.where(better, v[k], m1)
    first_is_0 = i1 == 0
    i2 = jnp.where(first_is_0, 1, 0)
    m2 = jnp.where(first_is_0, v[1], v[0])
    for k in range(1, n):
        better = jnp.logical_and(i1 != k, v[k] > m2)
        i2 = jnp.where(better, k, i2)
        m2 = jnp.where(better, v[k], m2)
    a1 = _pick(i1, a)
    a2 = _pick(i2, a)
    tot = a1 + a2
    return group * n + i1, group * n + i2, a1 / tot, a2 / tot


def _out_ln_kernel(layer, x_ref, w_hbm, h_ref, g_ref, b_ref, wr_ref, rb_ref,
                   of_ref, ob_ref, hp_ref, ri_ref, rw_ref, cnt_ref, stage_ref, wb_ref, sem):
    i = pl.program_id(0)

    @pl.when(i == 0)
    def _():
        cnt_ref[...] = jnp.zeros_like(cnt_ref)
        n_chunks = D_MODEL // W_CHUNK

        def w_copy(c):
            return pltpu.make_async_copy(w_hbm.at[layer, pl.ds(c * W_CHUNK, W_CHUNK), :],
                                         stage_ref.at[c % 2], sem.at[c % 2])

        w_copy(0).start()
        for c in range(n_chunks):
            if c + 1 < n_chunks:
                w_copy(c + 1).start()
            w_copy(c).wait()
            wb_ref[c * W_CHUNK:(c + 1) * W_CHUNK, :] = stage_ref[c % 2].astype(BF16)

    mix = jnp.dot(x_ref[...], wb_ref[...], preferred_element_type=F32)

    y = _layer_norm(ALPHA * h_ref[...] + mix, g_ref[...], b_ref[...])
    of_ref[...] = y
    ob_ref[...] = y.astype(BF16)
    hp_ref[...] = _pack_bf16_pairs(y)

    logits = lax.dot_general(wr_ref[...], y, (((1,), (1,)), ((), ())),
                             precision=lax.Precision.HIGHEST, preferred_element_type=F32)
    aff = jax.nn.sigmoid(logits)
    e1, e2, w1, w2 = _grouped_top2(aff + rb_ref[...], aff)
    t = TM_LN
    eid = lax.broadcasted_iota(jnp.int32, (N_EXPERTS, t), 0)
    hit1 = eid == e1
    hit2 = eid == e2
    onehot = jnp.where(jnp.logical_or(hit1, hit2), 1.0, 0.0)
    r = lax.broadcasted_iota(jnp.int32, (t, t), 0)
    c = lax.broadcasted_iota(jnp.int32, (t, t), 1)
    before = jnp.where(r < c, 1.0, 0.0).astype(BF16)
    seen = jnp.dot(onehot.astype(BF16), before, preferred_element_type=F32) + cnt_ref[:, 0:1]
    rank1 = jnp.sum(jnp.where(hit1, seen, 0.0), axis=0, keepdims=True)
    rank2 = jnp.sum(jnp.where(hit2, seen, 0.0), axis=0, keepdims=True)
    ri_ref[0:1, :] = e1
    ri_ref[1:2, :] = e2
    ri_ref[2:3, :] = rank1.astype(jnp.int32)
    ri_ref[3:4, :] = rank2.astype(jnp.int32)
    ri_ref[4:8, :] = jnp.zeros((4, t), jnp.int32)
    rw_ref[0:1, :] = w1
    rw_ref[1:2, :] = w2
    rw_ref[2:8, :] = jnp.zeros((6, t), F32)
    cnt_ref[...] += jnp.sum(onehot, axis=1, keepdims=True)


def _out_ln(merged, w_out, hf, g, b, wr_t, rb, layer):
    l, d = hf.shape
    row = pl.BlockSpec((TM_LN, d), lambda i: (i, 0))
    vec = pl.BlockSpec((None, 1, d), lambda i: (layer, 0, 0))
    tok = pl.BlockSpec((8, TM_LN), lambda i: (0, i))
    return pl.pallas_call(
        functools.partial(_out_ln_kernel, layer),
        grid=(l // TM_LN,),
        in_specs=[
            row,
            pl.BlockSpec(memory_space=pl.ANY),
            row, vec, vec,
            pl.BlockSpec((N_EXPERTS, d), lambda i: (0, 0)),
            pl.BlockSpec((N_EXPERTS, TM_LN), lambda i: (0, 0)),
        ],
        out_specs=[row, row, pl.BlockSpec((TM_LN, d // 2), lambda i: (i, 0)), tok, tok,
                   pl.BlockSpec((N_EXPERTS, LANE), lambda i: (0, 0))],
        out_shape=[jax.ShapeDtypeStruct((l, d), F32), jax.ShapeDtypeStruct((l, d), BF16),
                   jax.ShapeDtypeStruct((l, d // 2), jnp.uint32),
                   jax.ShapeDtypeStruct((8, l), jnp.int32), jax.ShapeDtypeStruct((8, l), F32),
                   jax.ShapeDtypeStruct((N_EXPERTS, LANE), F32)],
        scratch_shapes=[pltpu.VMEM((2, W_CHUNK, d), F32), pltpu.VMEM((d, d), BF16),
                        pltpu.SemaphoreType.DMA((2,))],
        compiler_params=_cparams(("arbitrary",)),
        name="out_ln",
    )(merged, w_out, hf, g.reshape(DEPTH, 1, d), b.reshape(DEPTH, 1, d), wr_t, rb)


def _invert_kernel(pos1_ref, pos2_ref, tok_ref):
    def clear(r, carry):
        tok_ref[r] = 0
        return carry

    lax.fori_loop(0, tok_ref.shape[0], clear, 0, unroll=8)

    def mark(t, carry):
        tok_ref[pos1_ref[t]] = t
        tok_ref[pos2_ref[t]] = t
        return carry

    lax.fori_loop(0, pos1_ref.shape[0], mark, 0, unroll=8)


def _invert(pos1, pos2, n_rows):
    smem = pl.BlockSpec(memory_space=pltpu.SMEM)
    return pl.pallas_call(
        _invert_kernel,
        in_specs=[smem, smem],
        out_specs=smem,
        out_shape=jax.ShapeDtypeStruct((n_rows,), jnp.int32),
        name="invert",
    )(pos1, pos2)


def _pack_bf16_pairs(y):
    bits = pltpu.bitcast(y.astype(BF16).astype(F32), jnp.uint32)
    half = y.shape[1] // 2
    return (bits[:, :half] >> 16) | (bits[:, half:] & jnp.uint32(0xFFFF0000))


def _unpack_bf16_pairs(words):
    return (pltpu.bitcast(words << 16, F32), pltpu.bitcast(words & jnp.uint32(0xFFFF0000), F32))


def _expert_kernel(be_ref, nu_ref, nv_ref, tok_ref, hp_hbm, wg_ref, wu_ref, wd_ref, o_ref,
                   xs_ref, xb_ref, acc_ref, wgb_ref, wub_ref, wdb_ref, sem):
    blk = pl.program_id(0)
    f = pl.program_id(1)
    nf = EXPERT_FF // MOE_TF
    used = blk < nu_ref[0]
    slot = blk % 2
    per_step = MOE_CHUNK // nf
    chunks = [slice(c * MOE_CHUNK, (c + 1) * MOE_CHUNK) for c in range(MOE_CHUNKS)]
    here = [nv_ref[blk] > c * MOE_CHUNK for c in range(MOE_CHUNKS)]
    ahead = [nv_ref[blk + 1] > c * MOE_CHUNK for c in range(MOE_CHUNKS)]

    def row_copy(s, r, src_row):
        return pltpu.make_async_copy(hp_hbm.at[pl.ds(src_row, 1)], xs_ref.at[s, pl.ds(r, 1)], sem.at[s])

    def issue_chunk(first_row):
        def issue(r, carry):
            row_copy(0, first_row + r, tok_ref[first_row + r]).start()
            return carry

        lax.fori_loop(0, MOE_CHUNK, issue, 0, unroll=8)

    def wait_chunk():
        def wait(r, carry):
            row_copy(slot, 0, 0).wait()
            return carry

        lax.fori_loop(0, MOE_CHUNK, wait, 0, unroll=8)

    def unpack(rows):
        lo, hi = _unpack_bf16_pairs(xs_ref[slot, rows, :])
        xb_ref[rows, :D_MODEL // 2] = lo.astype(BF16)
        xb_ref[rows, D_MODEL // 2:] = hi.astype(BF16)

    def compute(rows):
        x = xb_ref[rows, :]
        g = jnp.dot(x, wgb_ref[...], preferred_element_type=F32)
        u = jnp.dot(x, wub_ref[...], preferred_element_type=F32)
        a = (g * jax.nn.sigmoid(g) * u).astype(BF16)
        acc_ref[rows, :] += jnp.dot(a, wdb_ref[...], preferred_element_type=F32)

    for c in range(MOE_CHUNKS):
        @pl.when(jnp.logical_and(jnp.logical_and(blk == 0, f == 0), here[c]))
        def _():
            issue_chunk(c * MOE_CHUNK)

    for c in range(MOE_CHUNKS):
        @pl.when(jnp.logical_and(f == 0, here[c]))
        def _():
            wait_chunk()

    for c in range(MOE_CHUNKS):
        @pl.when(jnp.logical_and(f == 0, here[c]))
        def _():
            unpack(chunks[c])

    @pl.when(jnp.logical_and(used, f == 0))
    def _():
        acc_ref[...] = jnp.zeros_like(acc_ref)

    @pl.when(used)
    def _():
        wgb_ref[...] = wg_ref[...].astype(BF16)
        wub_ref[...] = wu_ref[...].astype(BF16)
        wdb_ref[...] = wd_ref[...].astype(BF16)

    for c in range(MOE_CHUNKS):
        @pl.when(jnp.logical_and(used, ahead[c]))
        def _():
            nxt = (blk + 1) * MOE_TM
            for r in range(per_step):
                row = c * MOE_CHUNK + f * per_step + r
                row_copy(1 - slot, row, tok_ref[nxt + row]).start()

        @pl.when(here[c])
        def _():
            compute(chunks[c])

    @pl.when(f == nf - 1)
    def _():
        o_ref[...] = _pack_bf16_pairs(jnp.where(used, acc_ref[...], 0.0))


def _expert_ffn(block_e, n_used, n_valid, row_tok, hp, w_gate, w_up, w_down, layer):
    d = D_MODEL
    nb = block_e.shape[0]
    nf = EXPERT_FF // MOE_TF

    def w_in_map(b, f, be, nu, nv, tok):
        return (layer, be[b], 0, jnp.where(b < nu[0], f, nf - 1))

    def w_out_map(b, f, be, nu, nv, tok):
        return (layer, be[b], jnp.where(b < nu[0], f, nf - 1), 0)

    grid_spec = pltpu.PrefetchScalarGridSpec(
        num_scalar_prefetch=4,
        grid=(nb, nf),
        in_specs=[
            pl.BlockSpec(memory_space=pl.ANY),
            pl.BlockSpec((None, None, d, MOE_TF), w_in_map),
            pl.BlockSpec((None, None, d, MOE_TF), w_in_map),
            pl.BlockSpec((None, None, MOE_TF, d), w_out_map),
        ],
        out_specs=pl.BlockSpec((MOE_TM, d // 2), lambda b, f, be, nu, nv, tok: (b, 0)),
        scratch_shapes=[pltpu.VMEM((2, MOE_TM, d // 2), jnp.uint32), pltpu.VMEM((MOE_TM, d), BF16),
                        pltpu.VMEM((MOE_TM, d), F32), pltpu.VMEM((d, MOE_TF), BF16),
                        pltpu.VMEM((d, MOE_TF), BF16), pltpu.VMEM((MOE_TF, d), BF16),
                        pltpu.SemaphoreType.DMA((2,))],
    )
    return pl.pallas_call(
        _expert_kernel,
        grid_spec=grid_spec,
        out_shape=jax.ShapeDtypeStruct((nb * MOE_TM, d // 2), jnp.uint32),
        compiler_params=_cparams(("arbitrary", "arbitrary")),
        name="expert_ffn",
    )(block_e, n_used, n_valid, row_tok, hp, w_gate, w_up, w_down)


def _combine_kernel(first_row, with_bf16, pos1_ref, pos2_ref, yb_hbm, h_ref, w_ref, g_ref, b_ref, *rest):
    out_refs, (y1_ref, y2_ref, sem) = rest[:-3], rest[-3:]
    i = pl.program_id(0)
    slot = i % 2

    def row_copy(dst, s, r, src_row):
        return pltpu.make_async_copy(yb_hbm.at[pl.ds(src_row, 1)], dst.at[s, pl.ds(r, 1)], sem.at[s])

    def issue_tile(tile, s):
        base = first_row + tile * CMB_TM

        def issue(r, carry):
            row_copy(y1_ref, s, r, pos1_ref[base + r]).start()
            row_copy(y2_ref, s, r, pos2_ref[base + r]).start()
            return carry

        lax.fori_loop(0, CMB_TM, issue, 0, unroll=8)

    @pl.when(i == 0)
    def _():
        issue_tile(0, 0)

    @pl.when(i + 1 < pl.num_programs(0))
    def _():
        issue_tile(i + 1, 1 - slot)

    def wait(r, carry):
        row_copy(y1_ref, slot, 0, 0).wait()
        row_copy(y2_ref, slot, 0, 0).wait()
        return carry

    lax.fori_loop(0, CMB_TM, wait, 0, unroll=8)
    w = w_ref[...]
    lo1, hi1 = _unpack_bf16_pairs(y1_ref[slot])
    lo2, hi2 = _unpack_bf16_pairs(y2_ref[slot])
    y = jnp.concatenate([w[:, 0:1] * lo1 + w[:, 1:2] * lo2, w[:, 0:1] * hi1 + w[:, 1:2] * hi2], axis=1)
    out = _layer_norm(ALPHA * h_ref[...] + y, g_ref[...], b_ref[...])
    out_refs[0][...] = out
    if with_bf16:
        out_refs[1][...] = out.astype(BF16)


def _combine_ln(pos1, pos2, yb, hf, wts, g, b, layer, final):
    l, d = hf.shape
    first_block = PREFIX // CMB_TM if final else 0
    n_rows = l - first_block * CMB_TM
    src = lambda i, p1, p2: (i + first_block, 0)
    dst = lambda i, p1, p2: (i, 0)
    vec = pl.BlockSpec((None, 1, d), lambda i, p1, p2: (layer, 0, 0))
    out_specs = [pl.BlockSpec((CMB_TM, d), dst)]
    out_shape = [jax.ShapeDtypeStruct((n_rows, d), F32)]
    if not final:
        out_specs.append(pl.BlockSpec((CMB_TM, d), dst))
        out_shape.append(jax.ShapeDtypeStruct((n_rows, d), BF16))
    grid_spec = pltpu.PrefetchScalarGridSpec(
        num_scalar_prefetch=2,
        grid=(n_rows // CMB_TM,),
        in_specs=[
            pl.BlockSpec(memory_space=pl.ANY),
            pl.BlockSpec((CMB_TM, d), src),
            pl.BlockSpec((CMB_TM, TOP_K), src),
            vec, vec,
        ],
        out_specs=out_specs,
        scratch_shapes=[
            pltpu.VMEM((2, CMB_TM, d // 2), jnp.uint32),
            pltpu.VMEM((2, CMB_TM, d // 2), jnp.uint32),
            pltpu.SemaphoreType.DMA((2,)),
        ],
    )
    return pl.pallas_call(
        functools.partial(_combine_kernel, first_block * CMB_TM, not final),
        grid_spec=grid_spec,
        out_shape=out_shape,
        compiler_params=_cparams(("arbitrary",)),
        name="combine_ln",
    )(pos1, pos2, yb, hf, wts, g.reshape(DEPTH, 1, d), b.reshape(DEPTH, 1, d))


def _plan(route_i, counts_f):
    n = route_i.shape[1]
    nb = -(-(n * TOP_K + N_EXPERTS * (MOE_TM - 1)) // MOE_TM)
    counts = counts_f[:, 0].astype(jnp.int32)
    blocks_e = (counts + MOE_TM - 1) // MOE_TM
    bend = jnp.cumsum(blocks_e)
    base = (bend - blocks_e) * MOE_TM
    eids = jnp.arange(N_EXPERTS, dtype=jnp.int32)[:, None]
    base_of = lambda e: jnp.sum(jnp.where(e[None, :] == eids, base[:, None], 0), axis=0)
    pos1 = (route_i[2] + base_of(route_i[0])).astype(jnp.int32)
    pos2 = (route_i[3] + base_of(route_i[1])).astype(jnp.int32)
    n_used = bend[-1:].astype(jnp.int32)
    blocks = jnp.arange(nb, dtype=jnp.int32)
    block_e = jnp.minimum(jnp.searchsorted(bend, blocks, side='right'), N_EXPERTS - 1).astype(jnp.int32)
    done = (blocks - (bend - blocks_e)[block_e]) * MOE_TM
    n_valid = jnp.where(blocks < n_used[0], jnp.clip(counts[block_e] - done, 0, MOE_TM), 0)
    n_valid = jnp.concatenate([n_valid, jnp.zeros((1,), jnp.int32)]).astype(jnp.int32)
    return pos1, pos2, block_e, n_used, n_valid


def kernel(x, meta_tokens, ln_in_g, ln_in_b, w_in, b_forget, w_branch_sb, w_branch_fox, w_out,
           ln_mix_g, ln_mix_b, w_router, router_bias, w_gate, w_up, w_down, ln_ffn_g, ln_ffn_b):
    b, s, d = x.shape
    assert b == 1 and d == D_MODEL
    l = s + PREFIX
    prefix = jnp.concatenate([jnp.zeros((N_PAD, d), x.dtype), meta_tokens.astype(x.dtype)], axis=0)
    hf, hb = _ln_in(prefix, x[0], ln_in_g, ln_in_b)

    w_in_t = jnp.swapaxes(w_in, 1, 2)
    wr_t = w_router.astype(F32).T
    rb = jnp.broadcast_to(router_bias.astype(F32)[:, None], (N_EXPERTS, TM_LN))
    for layer in range(DEPTH):
        proj = _in_proj(hb, w_in_t, layer)
        bf = jnp.broadcast_to(b_forget[layer].astype(F32)[:, None], (N_HEADS, TM_LN))
        key_bias = _forget_bias(hf, w_in_t, bf, layer).reshape(N_HEADS, l // FOX_T, 1, FOX_T)
        y_sb = _sb_attention(proj)
        y_fx = _fox_attention(proj, key_bias)
        merged = _merge(y_sb, y_fx, proj, w_branch_sb, w_branch_fox, layer)
        hf, hb, hp, route_i, route_w, counts = _out_ln(merged, w_out, hf, ln_mix_g, ln_mix_b,
                                                       wr_t, rb, layer)

        pos1, pos2, block_e, n_used, n_valid = _plan(route_i, counts)
        row_tok = _invert(pos1, pos2, block_e.shape[0] * MOE_TM)
        yb = _expert_ffn(block_e, n_used, n_valid, row_tok, hp, w_gate, w_up, w_down, layer)
        final = layer == DEPTH - 1
        outs = _combine_ln(pos1, pos2, yb, hf, route_w[:TOP_K].T, ln_ffn_g, ln_ffn_b, layer, final)
        if not final:
            hf, hb = outs
    return outs[0][None]
```

```python
import functools

import jax
import jax.numpy as jnp
from jax import lax
from jax.experimental import pallas as pl
from jax.experimental.pallas import tpu as pltpu

F32 = jnp.float32
BF16 = jnp.bfloat16

D_MODEL = 2048
DEPTH = 2
HEAD_DIM = 128
N_HEADS = 8
WIDTH = N_HEADS * HEAD_DIM
N_META = 16
N_PAD = 112
PREFIX = N_PAD + N_META
N_EXPERTS = 16
N_GROUPS = 4
EXPERTS_PER_GROUP = N_EXPERTS // N_GROUPS
TOP_K = 2
EXPERT_FF = 1024
ALPHA = (2 * DEPTH) ** 0.25
LN_EPS = 1e-5
NEG = -1e30
QK_SCALE = HEAD_DIM ** -0.5
QKV_COLS = 6 * WIDTH
F_COL0 = QKV_COLS
G_COL0 = QKV_COLS + N_HEADS

SB_EXIT = -104.0

LANE = 128
VMEM_LIMIT = 56 * 1024 * 1024
TM_PROJ = 1664
TN_PROJ = 512
N_QKV_TILES = QKV_COLS // TN_PROJ
TM_LN = 640
W_CHUNK = 256
SB_TQ = 128
SB_TK = 128
FOX_T = 640
FOX_GROUP = 4
MOE_CHUNK = 384
MOE_CHUNKS = 3
MOE_TM = MOE_CHUNK * MOE_CHUNKS
MOE_TF = 256
CMB_TM = 128


def _cparams(sem):
    return pltpu.CompilerParams(dimension_semantics=sem, vmem_limit_bytes=VMEM_LIMIT)


def _layer_norm(x, g, b):
    mu = jnp.mean(x, axis=-1, keepdims=True)
    xc = x - mu
    var = jnp.mean(xc * xc, axis=-1, keepdims=True)
    return xc * lax.rsqrt(var + LN_EPS) * g + b


def _log_sigmoid(z):
    return jnp.minimum(z, 0.0) - jnp.log(1.0 + jnp.exp(-jnp.abs(z)))


def _ln_in_kernel(p_ref, x_ref, g_ref, b_ref, of_ref, ob_ref):
    src = jnp.where(pl.program_id(0) == 0, p_ref[...], x_ref[...])
    y = _layer_norm(src, g_ref[...], b_ref[...])
    of_ref[...] = y
    ob_ref[...] = y.astype(BF16)


def _ln_in(prefix, x, g, b):
    s, d = x.shape
    l = s + PREFIX
    row = pl.BlockSpec((PREFIX, d), lambda i: (i, 0))
    vec = pl.BlockSpec((1, d), lambda i: (0, 0))
    return pl.pallas_call(
        _ln_in_kernel,
        grid=(l // PREFIX,),
        in_specs=[
            pl.BlockSpec((PREFIX, d), lambda i: (0, 0)),
            pl.BlockSpec((PREFIX, d), lambda i: (jnp.maximum(i - 1, 0), 0)),
            vec, vec,
        ],
        out_specs=[row, row],
        out_shape=[jax.ShapeDtypeStruct((l, d), F32), jax.ShapeDtypeStruct((l, d), BF16)],
        compiler_params=_cparams(("arbitrary",)),
        name="ln_in",
    )(prefix, x, g.reshape(1, d), b.reshape(1, d))


def _in_proj_kernel(layer, x_ref, wt_hbm, o_ref, stage_ref, wb_ref, sem):
    j = pl.program_id(0)
    i = pl.program_id(1)

    def w_copy(tile, slot):
        row0 = jnp.where(tile < N_QKV_TILES, tile * TN_PROJ, G_COL0 + (tile - N_QKV_TILES) * TN_PROJ)
        row0 = pl.multiple_of(row0, 8)
        return pltpu.make_async_copy(wt_hbm.at[layer, pl.ds(row0, TN_PROJ), :],
                                     stage_ref.at[slot], sem.at[slot])

    @pl.when(i == 0)
    def _():
        @pl.when(j == 0)
        def _():
            w_copy(0, 0).start()

        @pl.when(j + 1 < pl.num_programs(0))
        def _():
            w_copy(j + 1, (j + 1) % 2).start()

        w_copy(j, j % 2).wait()
        wb_ref[...] = stage_ref[j % 2].astype(BF16)

    acc = lax.dot_general(x_ref[...], wb_ref[...], (((1,), (1,)), ((), ())),
                          preferred_element_type=F32)

    @pl.when(j < N_QKV_TILES)
    def _():
        part = j % (3 * WIDTH // TN_PROJ)
        is_q = part < WIDTH // TN_PROJ
        is_v = part >= 2 * WIDTH // TN_PROJ
        y = acc * jnp.where(is_q, QK_SCALE, 1.0)
        rows = i * TM_PROJ + lax.broadcasted_iota(jnp.int32, acc.shape, 0)
        o_ref[...] = jnp.where(jnp.logical_and(is_v, rows < N_PAD), 0.0, y).astype(BF16)

    @pl.when(j >= N_QKV_TILES)
    def _():
        o_ref[...] = jax.nn.sigmoid(acc).astype(BF16)


def _in_proj(hb, w_in_t, layer):
    l, d = hb.shape
    n_tiles = N_QKV_TILES + 2 * D_MODEL // TN_PROJ
    return pl.pallas_call(
        functools.partial(_in_proj_kernel, layer),
        grid=(n_tiles, l // TM_PROJ),
        in_specs=[
            pl.BlockSpec((TM_PROJ, d), lambda j, i: (i, 0)),
            pl.BlockSpec(memory_space=pl.ANY),
        ],
        out_specs=pl.BlockSpec((TM_PROJ, TN_PROJ), lambda j, i: (i, j)),
        out_shape=jax.ShapeDtypeStruct((l, n_tiles * TN_PROJ), BF16),
        scratch_shapes=[pltpu.VMEM((2, TN_PROJ, d), F32), pltpu.VMEM((TN_PROJ, d), BF16),
                        pltpu.SemaphoreType.DMA((2,))],
        compiler_params=_cparams(("arbitrary", "arbitrary")),
        name="in_proj",
    )(hb, w_in_t)


def _forget_kernel(h_ref, w_ref, b_ref, o_ref, carry_ref):
    i = pl.program_id(0)

    @pl.when(i == 0)
    def _():
        carry_ref[...] = jnp.zeros_like(carry_ref)

    f = lax.dot_general(w_ref[...].astype(BF16), h_ref[...], (((1,), (1,)), ((), ())),
                        preferred_element_type=F32) + b_ref[...]
    pos = i * TM_LN + lax.broadcasted_iota(jnp.int32, f.shape, 1)
    lf = jnp.where(pos >= N_PAD, _log_sigmoid(f), 0.0)
    r = lax.broadcasted_iota(jnp.int32, (TM_LN, TM_LN), 0)
    c = lax.broadcasted_iota(jnp.int32, (TM_LN, TM_LN), 1)
    upto = jnp.where(r <= c, 1.0, 0.0).astype(BF16)
    hi = lf.astype(BF16)
    rest = lf - hi.astype(F32)
    mid = rest.astype(BF16)
    lo = (rest - mid.astype(F32)).astype(BF16)
    prefix = lambda part: jnp.dot(part, upto, preferred_element_type=F32)
    cs = (prefix(lo) + prefix(mid)) + prefix(hi) + carry_ref[:, 0:1]
    o_ref[...] = jnp.where(pos >= N_PAD, -cs, NEG)
    carry_ref[...] = jnp.broadcast_to(cs[:, TM_LN - 1:TM_LN], carry_ref.shape)


def _forget_bias(hb, w_in_t, bf, layer):
    l, d = hb.shape
    return pl.pallas_call(
        _forget_kernel,
        grid=(l // TM_LN,),
        in_specs=[
            pl.BlockSpec((TM_LN, d), lambda i: (i, 0)),
            pl.BlockSpec((None, N_HEADS, d), lambda i: (layer, F_COL0 // N_HEADS, 0)),
            pl.BlockSpec((N_HEADS, TM_LN), lambda i: (0, 0)),
        ],
        out_specs=pl.BlockSpec((N_HEADS, TM_LN), lambda i: (0, i)),
        out_shape=jax.ShapeDtypeStruct((N_HEADS, l), F32),
        scratch_shapes=[pltpu.VMEM((N_HEADS, LANE), F32)],
        compiler_params=_cparams(("arbitrary",)),
        name="forget_bias",
    )(hb, w_in_t, bf)


def _sb_kernel(q_ref, k_ref, v_ref, o_ref, run_ref, acc_ref):
    i = pl.program_id(0)
    tq, tk = SB_TQ, SB_TK
    r = lax.broadcasted_iota(jnp.int32, (2 * tk, 2 * tk), 0)
    c = lax.broadcasted_iota(jnp.int32, (2 * tk, 2 * tk), 1)
    suffix = jnp.where(jnp.logical_or(c >= tk, (r % tk) > c), 1.0, 0.0).astype(BF16)
    row = lax.broadcasted_iota(jnp.int32, (tq, tk), 0)
    col = lax.broadcasted_iota(jnp.int32, (tq, tk), 1)
    strictly_causal = col < row

    def block(j, diagonal):
        start = pl.multiple_of(j * tk, tk)
        heads = range(N_HEADS)
        cols = [slice(hd * HEAD_DIM, (hd + 1) * HEAD_DIM) for hd in heads]
        zs = [lax.dot_general(q_ref[:, c], k_ref[pl.ds(start, tk), c], (((1,), (1,)), ((), ())),
                              preferred_element_type=F32) for c in cols]
        log_betas, cats = [], []
        for z in zs:
            log_beta = _log_sigmoid(z)
            log_keep = log_beta - z
            if diagonal:
                log_keep = jnp.where(strictly_causal, log_keep, 0.0)
            hi = log_keep.astype(BF16)
            lo = (log_keep - hi.astype(F32)).astype(BF16)
            log_betas.append(log_beta)
            cats.append(jnp.concatenate([hi, lo], axis=1))
        sums = [jnp.dot(cat, suffix, preferred_element_type=F32) for cat in cats]
        ws, worst = [], None
        for c, log_beta, s in zip(cols, log_betas, sums):
            if diagonal:
                w = jnp.where(strictly_causal, jnp.exp(log_beta + s[:, :tk]), 0.0)
                run = s[:, tk:]
            else:
                run = run_ref[:, c]
                w = jnp.exp(log_beta + s[:, :tk] + run)
                run = run + s[:, tk:]
            run_ref[:, c] = run
            ws.append(w.astype(BF16))
            worst = run if worst is None else jnp.maximum(worst, run)
        for c, w in zip(cols, ws):
            pv = jnp.dot(w, v_ref[pl.ds(start, tk), c], preferred_element_type=F32)
            if diagonal:
                acc_ref[:, c] = pv
            else:
                acc_ref[:, c] += pv
        return jnp.max(worst)

    worst = block(i, True)

    def cond(s):
        j, worst = s
        return jnp.logical_and(j >= 0, worst > SB_EXIT)

    def body(s):
        j, _ = s
        return j - 1, block(j, False)

    lax.while_loop(cond, body, (i - 1, worst))
    o_ref[...] = acc_ref[...].astype(BF16)


def _sb_attention(proj):
    l = proj.shape[0]
    resident = lambda blk: pl.BlockSpec((l, WIDTH), lambda i: (0, blk), pipeline_mode=pl.Buffered(1))
    return pl.pallas_call(
        _sb_kernel,
        grid=(l // SB_TQ,),
        in_specs=[pl.BlockSpec((SB_TQ, WIDTH), lambda i: (i, 0)), resident(1), resident(2)],
        out_specs=pl.BlockSpec((SB_TQ, WIDTH), lambda i: (i, 0)),
        out_shape=jax.ShapeDtypeStruct((l, WIDTH), BF16),
        scratch_shapes=[pltpu.VMEM((SB_TQ, WIDTH), F32), pltpu.VMEM((SB_TQ, WIDTH), F32)],
        compiler_params=_cparams(("arbitrary",)),
        name="sb_attention",
    )(proj, proj, proj)


def _fox_kernel(q_ref, k_ref, v_ref, b_ref, o_ref, vaug_ref):
    i = pl.program_id(1)
    t = FOX_T

    @pl.when(i == 0)
    def _():
        lane = lax.broadcasted_iota(jnp.int32, (vaug_ref.shape[1], HEAD_DIM), 1)
        ones_col = jnp.where(lane == 0, 1.0, 0.0).astype(BF16)
        for hd in range(FOX_GROUP):
            vaug_ref[hd, :, :HEAD_DIM] = v_ref[:, hd * HEAD_DIM:(hd + 1) * HEAD_DIM]
            vaug_ref[hd, :, HEAD_DIM:] = ones_col

    row = lax.broadcasted_iota(jnp.int32, (t, t), 0)
    col = lax.broadcasted_iota(jnp.int32, (t, t), 1)
    causal = col <= row

    def block(j, state, diagonal):
        start = pl.multiple_of(j * t, t)
        heads = range(FOX_GROUP)
        cols = [slice(hd * HEAD_DIM, (hd + 1) * HEAD_DIM) for hd in heads]
        scores = [lax.dot_general(q_ref[:, c], k_ref[pl.ds(start, t), c], (((1,), (1,)), ((), ())),
                                  preferred_element_type=F32) for c in cols]
        ps, scales, ms = [], [], []
        for hd, s in zip(heads, scores):
            m = state[hd][0]
            s = s + b_ref[hd, j]
            if diagonal:
                s = jnp.where(causal, s, NEG)
            m_new = jnp.maximum(m, jnp.max(s, axis=1, keepdims=True))
            ps.append(jnp.exp(s - m_new).astype(BF16))
            scales.append(jnp.exp(m - m_new))
            ms.append(m_new)
        out = []
        for hd in heads:
            pv = jnp.dot(ps[hd], vaug_ref[hd, pl.ds(start, t), :], preferred_element_type=F32)
            out.append((ms[hd], scales[hd] * state[hd][1] + pv))
        return tuple(out)

    init = tuple((jnp.full((t, 1), NEG, F32), jnp.zeros((t, 2 * HEAD_DIM), F32))
                 for _ in range(FOX_GROUP))
    state = block(i, init, True)
    state = lax.fori_loop(0, i, lambda j, s: block(j, s, False), state)
    for hd, (_, acc) in enumerate(state):
        o_ref[:, hd * HEAD_DIM:(hd + 1) * HEAD_DIM] = (
            acc[:, :HEAD_DIM] / acc[:, HEAD_DIM:HEAD_DIM + 1]).astype(BF16)


def _fox_attention(proj, key_bias):
    l = proj.shape[0]
    gw = FOX_GROUP * HEAD_DIM
    first = 3 * WIDTH // gw
    per = WIDTH // gw
    nkb = l // FOX_T
    return pl.pallas_call(
        _fox_kernel,
        grid=(N_HEADS // FOX_GROUP, l // FOX_T),
        in_specs=[
            pl.BlockSpec((FOX_T, gw), lambda g, i: (i, first + g)),
            pl.BlockSpec((l, gw), lambda g, i: (0, first + per + g), pipeline_mode=pl.Buffered(1)),
            pl.BlockSpec((l, gw), lambda g, i: (0, first + 2 * per + g), pipeline_mode=pl.Buffered(1)),
            pl.BlockSpec((FOX_GROUP, nkb, 1, FOX_T), lambda g, i: (g, 0, 0, 0)),
        ],
        out_specs=pl.BlockSpec((FOX_T, gw), lambda g, i: (i, g)),
        out_shape=jax.ShapeDtypeStruct((l, WIDTH), BF16),
        scratch_shapes=[pltpu.VMEM((FOX_GROUP, l, 2 * HEAD_DIM), BF16)],
        compiler_params=_cparams(("arbitrary", "arbitrary")),
        name="fox_attention",
    )(proj, proj, proj, key_bias)


def _merge_kernel(ys_ref, yf_ref, gs_ref, gf_ref, ws_ref, wf_ref, o_ref, wsb_ref, wfb_ref):
    i = pl.program_id(1)

    @pl.when(i == 0)
    def _():
        wsb_ref[...] = ws_ref[...].astype(BF16)
        wfb_ref[...] = wf_ref[...].astype(BF16)

    a = jnp.dot(ys_ref[...], wsb_ref[...], preferred_element_type=F32)
    b = jnp.dot(yf_ref[...], wfb_ref[...], preferred_element_type=F32)
    o_ref[...] = (gs_ref[...].astype(F32) * a + gf_ref[...].astype(F32) * b).astype(BF16)


def _merge(y_sb, y_fx, proj, w_bsb, w_bfx, layer):
    l = y_sb.shape[0]
    d = D_MODEL
    nj = d // TN_PROJ
    act = pl.BlockSpec((TM_PROJ, WIDTH), lambda j, i: (i, 0))
    wspec = pl.BlockSpec((None, WIDTH, TN_PROJ), lambda j, i: (layer, 0, j))
    return pl.pallas_call(
        _merge_kernel,
        grid=(nj, l // TM_PROJ),
        in_specs=[
            act, act,
            pl.BlockSpec((TM_PROJ, TN_PROJ), lambda j, i: (i, N_QKV_TILES + j)),
            pl.BlockSpec((TM_PROJ, TN_PROJ), lambda j, i: (i, N_QKV_TILES + nj + j)),
            wspec, wspec,
        ],
        out_specs=pl.BlockSpec((TM_PROJ, TN_PROJ), lambda j, i: (i, j)),
        out_shape=jax.ShapeDtypeStruct((l, d), BF16),
        scratch_shapes=[pltpu.VMEM((WIDTH, TN_PROJ), BF16), pltpu.VMEM((WIDTH, TN_PROJ), BF16)],
        compiler_params=_cparams(("arbitrary", "arbitrary")),
        name="merge",
    )(y_sb, y_fx, proj, proj, w_bsb, w_bfx)


def _pick(index, values):
    out = values[0]
    for k in range(1, len(values)):
        out = jnp.where(index == k, values[k], out)
    return out


def _grouped_top2(sel, aff):
    srow = [sel[e:e + 1, :] for e in range(N_EXPERTS)]
    arow = [aff[e:e + 1, :] for e in range(N_EXPERTS)]
    n = EXPERTS_PER_GROUP
    score = []
    for g in range(N_GROUPS):
        v = srow[g * n:(g + 1) * n]
        pair = [v[a] + v[b] for a in range(n) for b in range(a + 1, n)]
        score.append(functools.reduce(jnp.maximum, pair))
    group = jnp.zeros_like(score[0], dtype=jnp.int32)
    best = score[0]
    for g in range(1, N_GROUPS):
        better = score[g] > best
        group = jnp.where(better, g, group)
        best = jnp.where(better, score[g], best)
    v = [_pick(group, [srow[g * n + k] for g in range(N_GROUPS)]) for k in range(n)]
    a = [_pick(group, [arow[g * n + k] for g in range(N_GROUPS)]) for k in range(n)]
    i1 = jnp.zeros_like(group)
    m1 = v[0]
    for k in range(1, n):
        better = v[k] > m1
        i1 = jnp.where(better, k, i1)
        m1 = jnp.where(better, v[k], m1)
    first_is_0 = i1 == 0
    i2 = jnp.where(first_is_0, 1, 0)
    m2 = jnp.where(first_is_0, v[1], v[0])
    for k in range(1, n):
        better = jnp.logical_and(i1 != k, v[k] > m2)
        i2 = jnp.where(better, k, i2)
        m2 = jnp.where(better, v[k], m2)
    a1 = _pick(i1, a)
    a2 = _pick(i2, a)
    tot = a1 + a2
    return group * n + i1, group * n + i2, a1 / tot, a2 / tot


def _out_ln_kernel(layer, x_ref, w_hbm, h_ref, g_ref, b_ref, wr_ref, rb_ref,
                   of_ref, ob_ref, hp_ref, ri_ref, rw_ref, cnt_ref, stage_ref, wb_ref, sem):
    i = pl.program_id(0)

    @pl.when(i == 0)
    def _():
        cnt_ref[...] = jnp.zeros_like(cnt_ref)
        n_chunks = D_MODEL // W_CHUNK

        def w_copy(c):
            return pltpu.make_async_copy(w_hbm.at[layer, pl.ds(c * W_CHUNK, W_CHUNK), :],
                                         stage_ref.at[c % 2], sem.at[c % 2])

        w_copy(0).start()
        for c in range(n_chunks):
            if c + 1 < n_chunks:
                w_copy(c + 1).start()
            w_copy(c).wait()
            wb_ref[c * W_CHUNK:(c + 1) * W_CHUNK, :] = stage_ref[c % 2].astype(BF16)

    mix = jnp.dot(x_ref[...], wb_ref[...], preferred_element_type=F32)

    y = _layer_norm(ALPHA * h_ref[...] + mix, g_ref[...], b_ref[...])
    of_ref[...] = y
    ob_ref[...] = y.astype(BF16)
    hp_ref[...] = _pack_bf16_pairs(y)

    logits = lax.dot_general(wr_ref[...], y, (((1,), (1,)), ((), ())),
                             precision=lax.Precision.HIGHEST, preferred_element_type=F32)
    aff = jax.nn.sigmoid(logits)
    e1, e2, w1, w2 = _grouped_top2(aff + rb_ref[...], aff)
    t = TM_LN
    eid = lax.broadcasted_iota(jnp.int32, (N_EXPERTS, t), 0)
    hit1 = eid == e1
    hit2 = eid == e2
    onehot = jnp.where(jnp.logical_or(hit1, hit2), 1.0, 0.0)
    r = lax.broadcasted_iota(jnp.int32, (t, t), 0)
    c = lax.broadcasted_iota(jnp.int32, (t, t), 1)
    before = jnp.where(r < c, 1.0, 0.0).astype(BF16)
    seen = jnp.dot(onehot.astype(BF16), before, preferred_element_type=F32) + cnt_ref[:, 0:1]
    rank1 = jnp.sum(jnp.where(hit1, seen, 0.0), axis=0, keepdims=True)
    rank2 = jnp.sum(jnp.where(hit2, seen, 0.0), axis=0, keepdims=True)
    ri_ref[0:1, :] = e1
    ri_ref[1:2, :] = e2
    ri_ref[2:3, :] = rank1.astype(jnp.int32)
    ri_ref[3:4, :] = rank2.astype(jnp.int32)
    ri_ref[4:8, :] = jnp.zeros((4, t), jnp.int32)
    rw_ref[0:1, :] = w1
    rw_ref[1:2, :] = w2
    rw_ref[2:8, :] = jnp.zeros((6, t), F32)
    cnt_ref[...] += jnp.sum(onehot, axis=1, keepdims=True)


def _out_ln(merged, w_out, hf, g, b, wr_t, rb, layer):
    l, d = hf.shape
    row = pl.BlockSpec((TM_LN, d), lambda i: (i, 0))
    vec = pl.BlockSpec((None, 1, d), lambda i: (layer, 0, 0))
    tok = pl.BlockSpec((8, TM_LN), lambda i: (0, i))
    return pl.pallas_call(
        functools.partial(_out_ln_kernel, layer),
        grid=(l // TM_LN,),
        in_specs=[
            row,
            pl.BlockSpec(memory_space=pl.ANY),
            row, vec, vec,
            pl.BlockSpec((N_EXPERTS, d), lambda i: (0, 0)),
            pl.BlockSpec((N_EXPERTS, TM_LN), lambda i: (0, 0)),
        ],
        out_specs=[row, row, pl.BlockSpec((TM_LN, d // 2), lambda i: (i, 0)), tok, tok,
                   pl.BlockSpec((N_EXPERTS, LANE), lambda i: (0, 0))],
        out_shape=[jax.ShapeDtypeStruct((l, d), F32), jax.ShapeDtypeStruct((l, d), BF16),
                   jax.ShapeDtypeStruct((l, d // 2), jnp.uint32),
                   jax.ShapeDtypeStruct((8, l), jnp.int32), jax.ShapeDtypeStruct((8, l), F32),
                   jax.ShapeDtypeStruct((N_EXPERTS, LANE), F32)],
        scratch_shapes=[pltpu.VMEM((2, W_CHUNK, d), F32), pltpu.VMEM((d, d), BF16),
                        pltpu.SemaphoreType.DMA((2,))],
        compiler_params=_cparams(("arbitrary",)),
        name="out_ln",
    )(merged, w_out, hf, g.reshape(DEPTH, 1, d), b.reshape(DEPTH, 1, d), wr_t, rb)


def _invert_kernel(pos1_ref, pos2_ref, tok_ref):
    def clear(r, carry):
        tok_ref[r] = 0
        return carry

    lax.fori_loop(0, tok_ref.shape[0], clear, 0, unroll=8)

    def mark(t, carry):
        tok_ref[pos1_ref[t]] = t
        tok_ref[pos2_ref[t]] = t
        return carry

    lax.fori_loop(0, pos1_ref.shape[0], mark, 0, unroll=8)


def _invert(pos1, pos2, n_rows):
    smem = pl.BlockSpec(memory_space=pltpu.SMEM)
    return pl.pallas_call(
        _invert_kernel,
        in_specs=[smem, smem],
        out_specs=smem,
        out_shape=jax.ShapeDtypeStruct((n_rows,), jnp.int32),
        name="invert",
    )(pos1, pos2)


def _pack_bf16_pairs(y):
    bits = pltpu.bitcast(y.astype(BF16).astype(F32), jnp.uint32)
    half = y.shape[1] // 2
    return (bits[:, :half] >> 16) | (bits[:, half:] & jnp.uint32(0xFFFF0000))


def _unpack_bf16_pairs(words):
    return (pltpu.bitcast(words << 16, F32), pltpu.bitcast(words & jnp.uint32(0xFFFF0000), F32))


def _expert_kernel(be_ref, nu_ref, nv_ref, tok_ref, hp_hbm, wg_ref, wu_ref, wd_ref, o_ref,
                   xs_ref, xb_ref, acc_ref, wgb_ref, wub_ref, wdb_ref, sem):
    blk = pl.program_id(0)
    f = pl.program_id(1)
    nf = EXPERT_FF // MOE_TF
    used = blk < nu_ref[0]
    slot = blk % 2
    per_step = MOE_CHUNK // nf
    chunks = [slice(c * MOE_CHUNK, (c + 1) * MOE_CHUNK) for c in range(MOE_CHUNKS)]
    here = [nv_ref[blk] > c * MOE_CHUNK for c in range(MOE_CHUNKS)]
    ahead = [nv_ref[blk + 1] > c * MOE_CHUNK for c in range(MOE_CHUNKS)]

    def row_copy(s, r, src_row):
        return pltpu.make_async_copy(hp_hbm.at[pl.ds(src_row, 1)], xs_ref.at[s, pl.ds(r, 1)], sem.at[s])

    def issue_chunk(first_row):
        def issue(r, carry):
            row_copy(0, first_row + r, tok_ref[first_row + r]).start()
            return carry

        lax.fori_loop(0, MOE_CHUNK, issue, 0, unroll=8)

    def wait_chunk():
        def wait(r, carry):
            row_copy(slot, 0, 0).wait()
            return carry

        lax.fori_loop(0, MOE_CHUNK, wait, 0, unroll=8)

    def unpack(rows):
        lo, hi = _unpack_bf16_pairs(xs_ref[slot, rows, :])
        xb_ref[rows, :D_MODEL // 2] = lo.astype(BF16)
        xb_ref[rows, D_MODEL // 2:] = hi.astype(BF16)

    def compute(rows):
        x = xb_ref[rows, :]
        g = jnp.dot(x, wgb_ref[...], preferred_element_type=F32)
        u = jnp.dot(x, wub_ref[...], preferred_element_type=F32)
        a = (g * jax.nn.sigmoid(g) * u).astype(BF16)
        acc_ref[rows, :] += jnp.dot(a, wdb_ref[...], preferred_element_type=F32)

    for c in range(MOE_CHUNKS):
        @pl.when(jnp.logical_and(jnp.logical_and(blk == 0, f == 0), here[c]))
        def _():
            issue_chunk(c * MOE_CHUNK)

    for c in range(MOE_CHUNKS):
        @pl.when(jnp.logical_and(f == 0, here[c]))
        def _():
            wait_chunk()

    for c in range(MOE_CHUNKS):
        @pl.when(jnp.logical_and(f == 0, here[c]))
        def _():
            unpack(chunks[c])

    @pl.when(jnp.logical_and(used, f == 0))
    def _():
        acc_ref[...] = jnp.zeros_like(acc_ref)

    @pl.when(used)
    def _():
        wgb_ref[...] = wg_ref[...].astype(BF16)
        wub_ref[...] = wu_ref[...].astype(BF16)
        wdb_ref[...] = wd_ref[...].astype(BF16)

    for c in range(MOE_CHUNKS):
        @pl.when(jnp.logical_and(used, ahead[c]))
        def _():
            nxt = (blk + 1) * MOE_TM
            for r in range(per_step):
                row = c * MOE_CHUNK + f * per_step + r
                row_copy(1 - slot, row, tok_ref[nxt + row]).start()

        @pl.when(here[c])
        def _():
            compute(chunks[c])

    @pl.when(f == nf - 1)
    def _():
        o_ref[...] = _pack_bf16_pairs(jnp.where(used, acc_ref[...], 0.0))


def _expert_ffn(block_e, n_used, n_valid, row_tok, hp, w_gate, w_up, w_down, layer):
    d = D_MODEL
    nb = block_e.shape[0]
    nf = EXPERT_FF // MOE_TF

    def w_in_map(b, f, be, nu, nv, tok):
        return (layer, be[b], 0, jnp.where(b < nu[0], f, nf - 1))

    def w_out_map(b, f, be, nu, nv, tok):
        return (layer, be[b], jnp.where(b < nu[0], f, nf - 1), 0)

    grid_spec = pltpu.PrefetchScalarGridSpec(
        num_scalar_prefetch=4,
        grid=(nb, nf),
        in_specs=[
            pl.BlockSpec(memory_space=pl.ANY),
            pl.BlockSpec((None, None, d, MOE_TF), w_in_map),
            pl.BlockSpec((None, None, d, MOE_TF), w_in_map),
            pl.BlockSpec((None, None, MOE_TF, d), w_out_map),
        ],
        out_specs=pl.BlockSpec((MOE_TM, d // 2), lambda b, f, be, nu, nv, tok: (b, 0)),
        scratch_shapes=[pltpu.VMEM((2, MOE_TM, d // 2), jnp.uint32), pltpu.VMEM((MOE_TM, d), BF16),
                        pltpu.VMEM((MOE_TM, d), F32), pltpu.VMEM((d, MOE_TF), BF16),
                        pltpu.VMEM((d, MOE_TF), BF16), pltpu.VMEM((MOE_TF, d), BF16),
                        pltpu.SemaphoreType.DMA((2,))],
    )
    return pl.pallas_call(
        _expert_kernel,
        grid_spec=grid_spec,
        out_shape=jax.ShapeDtypeStruct((nb * MOE_TM, d // 2), jnp.uint32),
        compiler_params=_cparams(("arbitrary", "arbitrary")),
        name="expert_ffn",
    )(block_e, n_used, n_valid, row_tok, hp, w_gate, w_up, w_down)


def _combine_kernel(first_row, with_bf16, pos1_ref, pos2_ref, yb_hbm, h_ref, w_ref, g_ref, b_ref, *rest):
    out_refs, (y1_ref, y2_ref, sem) = rest[:-3], rest[-3:]
    i = pl.program_id(0)
    slot = i % 2

    def row_copy(dst, s, r, src_row):
        return pltpu.make_async_copy(yb_hbm.at[pl.ds(src_row, 1)], dst.at[s, pl.ds(r, 1)], sem.at[s])

    def issue_tile(tile, s):
        base = first_row + tile * CMB_TM

        def issue(r, carry):
            row_copy(y1_ref, s, r, pos1_ref[base + r]).start()
            row_copy(y2_ref, s, r, pos2_ref[base + r]).start()
            return carry

        lax.fori_loop(0, CMB_TM, issue, 0, unroll=8)

    @pl.when(i == 0)
    def _():
        issue_tile(0, 0)

    @pl.when(i + 1 < pl.num_programs(0))
    def _():
        issue_tile(i + 1, 1 - slot)

    def wait(r, carry):
        row_copy(y1_ref, slot, 0, 0).wait()
        row_copy(y2_ref, slot, 0, 0).wait()
        return carry

    lax.fori_loop(0, CMB_TM, wait, 0, unroll=8)
    w = w_ref[...]
    lo1, hi1 = _unpack_bf16_pairs(y1_ref[slot])
    lo2, hi2 = _unpack_bf16_pairs(y2_ref[slot])
    y = jnp.concatenate([w[:, 0:1] * lo1 + w[:, 1:2] * lo2, w[:, 0:1] * hi1 + w[:, 1:2] * hi2], axis=1)
    out = _layer_norm(ALPHA * h_ref[...] + y, g_ref[...], b_ref[...])
    out_refs[0][...] = out
    if with_bf16:
        out_refs[1][...] = out.astype(BF16)


def _combine_ln(pos1, pos2, yb, hf, wts, g, b, layer, final):
    l, d = hf.shape
    first_block = PREFIX // CMB_TM if final else 0
    n_rows = l - first_block * CMB_TM
    src = lambda i, p1, p2: (i + first_block, 0)
    dst = lambda i, p1, p2: (i, 0)
    vec = pl.BlockSpec((None, 1, d), lambda i, p1, p2: (layer, 0, 0))
    out_specs = [pl.BlockSpec((CMB_TM, d), dst)]
    out_shape = [jax.ShapeDtypeStruct((n_rows, d), F32)]
    if not final:
        out_specs.append(pl.BlockSpec((CMB_TM, d), dst))
        out_shape.append(jax.ShapeDtypeStruct((n_rows, d), BF16))
    grid_spec = pltpu.PrefetchScalarGridSpec(
        num_scalar_prefetch=2,
        grid=(n_rows // CMB_TM,),
        in_specs=[
            pl.BlockSpec(memory_space=pl.ANY),
            pl.BlockSpec((CMB_TM, d), src),
            pl.BlockSpec((CMB_TM, TOP_K), src),
            vec, vec,
        ],
        out_specs=out_specs,
        scratch_shapes=[
            pltpu.VMEM((2, CMB_TM, d // 2), jnp.uint32),
            pltpu.VMEM((2, CMB_TM, d // 2), jnp.uint32),
            pltpu.SemaphoreType.DMA((2,)),
        ],
    )
    return pl.pallas_call(
        functools.partial(_combine_kernel, first_block * CMB_TM, not final),
        grid_spec=grid_spec,
        out_shape=out_shape,
        compiler_params=_cparams(("arbitrary",)),
        name="combine_ln",
    )(pos1, pos2, yb, hf, wts, g.reshape(DEPTH, 1, d), b.reshape(DEPTH, 1, d))


def _plan(route_i, counts_f):
    n = route_i.shape[1]
    nb = -(-(n * TOP_K + N_EXPERTS * (MOE_TM - 1)) // MOE_TM)
    counts = counts_f[:, 0].astype(jnp.int32)
    blocks_e = (counts + MOE_TM - 1) // MOE_TM
    bend = jnp.cumsum(blocks_e)
    base = (bend - blocks_e) * MOE_TM
    eids = jnp.arange(N_EXPERTS, dtype=jnp.int32)[:, None]
    base_of = lambda e: jnp.sum(jnp.where(e[None, :] == eids, base[:, None], 0), axis=0)
    pos1 = (route_i[2] + base_of(route_i[0])).astype(jnp.int32)
    pos2 = (route_i[3] + base_of(route_i[1])).astype(jnp.int32)
    n_used = bend[-1:].astype(jnp.int32)
    blocks = jnp.arange(nb, dtype=jnp.int32)
    block_e = jnp.minimum(jnp.searchsorted(bend, blocks, side='right'), N_EXPERTS - 1).astype(jnp.int32)
    done = (blocks - (bend - blocks_e)[block_e]) * MOE_TM
    n_valid = jnp.where(blocks < n_used[0], jnp.clip(counts[block_e] - done, 0, MOE_TM), 0)
    n_valid = jnp.concatenate([n_valid, jnp.zeros((1,), jnp.int32)]).astype(jnp.int32)
    return pos1, pos2, block_e, n_used, n_valid


def kernel(x, meta_tokens, ln_in_g, ln_in_b, w_in, b_forget, w_branch_sb, w_branch_fox, w_out,
           ln_mix_g, ln_mix_b, w_router, router_bias, w_gate, w_up, w_down, ln_ffn_g, ln_ffn_b):
    b, s, d = x.shape
    assert b == 1 and d == D_MODEL
    l = s + PREFIX
    prefix = jnp.concatenate([jnp.zeros((N_PAD, d), x.dtype), meta_tokens.astype(x.dtype)], axis=0)
    hf, hb = _ln_in(prefix, x[0], ln_in_g, ln_in_b)

    w_in_t = jnp.swapaxes(w_in, 1, 2)
    wr_t = w_router.astype(F32).T
    rb = jnp.broadcast_to(router_bias.astype(F32)[:, None], (N_EXPERTS, TM_LN))
    for layer in range(DEPTH):
        proj = _in_proj(hb, w_in_t, layer)
        bf = jnp.broadcast_to(b_forget[layer].astype(F32)[:, None], (N_HEADS, TM_LN))
        key_bias = _forget_bias(hb, w_in_t, bf, layer).reshape(N_HEADS, l // FOX_T, 1, FOX_T)
        y_sb = _sb_attention(proj)
        y_fx = _fox_attention(proj, key_bias)
        merged = _merge(y_sb, y_fx, proj, w_branch_sb, w_branch_fox, layer)
        hf, hb, hp, route_i, route_w, counts = _out_ln(merged, w_out, hf, ln_mix_g, ln_mix_b,
                                                       wr_t, rb, layer)

        pos1, pos2, block_e, n_used, n_valid = _plan(route_i, counts)
        row_tok = _invert(pos1, pos2, block_e.shape[0] * MOE_TM)
        yb = _expert_ffn(block_e, n_used, n_valid, row_tok, hp, w_gate, w_up, w_down, layer)
        final = layer == DEPTH - 1
        outs = _combine_ln(pos1, pos2, yb, hf, route_w[:TOP_K].T, ln_ffn_g, ln_ffn_b, layer, final)
        if not final:
            hf, hb = outs
    return outs[0][None]
```

```python
import functools

import jax
import jax.numpy as jnp
from jax import lax
from jax.experimental import pallas as pl
from jax.experimental.pallas import tpu as pltpu

F32 = jnp.float32
BF16 = jnp.bfloat16

D_MODEL = 2048
DEPTH = 2
HEAD_DIM = 128
N_HEADS = 8
WIDTH = N_HEADS * HEAD_DIM
N_META = 16
N_PAD = 112
PREFIX = N_PAD + N_META
N_EXPERTS = 16
N_GROUPS = 4
EXPERTS_PER_GROUP = N_EXPERTS // N_GROUPS
TOP_K = 2
EXPERT_FF = 1024
ALPHA = (2 * DEPTH) ** 0.25
LN_EPS = 1e-5
NEG = -1e30
QK_SCALE = HEAD_DIM ** -0.5
QKV_COLS = 6 * WIDTH
F_COL0 = QKV_COLS
G_COL0 = QKV_COLS + N_HEADS

SB_EXIT = -104.0

LANE = 128
VMEM_LIMIT = 56 * 1024 * 1024
TM_PROJ = 1664
TN_PROJ = 512
N_QKV_TILES = QKV_COLS // TN_PROJ
TM_LN = 640
W_CHUNK = 256
SB_TQ = 128
SB_TK = 128
FOX_T = 640
FOX_GROUP = 4
MOE_CHUNK = 384
MOE_CHUNKS = 3
MOE_TM = MOE_CHUNK * MOE_CHUNKS
MOE_TF = 256
CMB_TM = 128


def _cparams(sem):
    return pltpu.CompilerParams(dimension_semantics=sem, vmem_limit_bytes=VMEM_LIMIT)


def _layer_norm(x, g, b):
    mu = jnp.mean(x, axis=-1, keepdims=True)
    xc = x - mu
    var = jnp.mean(xc * xc, axis=-1, keepdims=True)
    return xc * lax.rsqrt(var + LN_EPS) * g + b


def _log_sigmoid(z):
    return jnp.minimum(z, 0.0) - jnp.log(1.0 + jnp.exp(-jnp.abs(z)))


def _ln_in_kernel(p_ref, x_ref, g_ref, b_ref, of_ref, ob_ref):
    src = jnp.where(pl.program_id(0) == 0, p_ref[...], x_ref[...])
    y = _layer_norm(src, g_ref[...], b_ref[...])
    of_ref[...] = y
    ob_ref[...] = y.astype(BF16)


def _ln_in(prefix, x, g, b):
    s, d = x.shape
    l = s + PREFIX
    row = pl.BlockSpec((PREFIX, d), lambda i: (i, 0))
    vec = pl.BlockSpec((1, d), lambda i: (0, 0))
    return pl.pallas_call(
        _ln_in_kernel,
        grid=(l // PREFIX,),
        in_specs=[
            pl.BlockSpec((PREFIX, d), lambda i: (0, 0)),
            pl.BlockSpec((PREFIX, d), lambda i: (jnp.maximum(i - 1, 0), 0)),
            vec, vec,
        ],
        out_specs=[row, row],
        out_shape=[jax.ShapeDtypeStruct((l, d), F32), jax.ShapeDtypeStruct((l, d), BF16)],
        compiler_params=_cparams(("arbitrary",)),
        name="ln_in",
    )(prefix, x, g.reshape(1, d), b.reshape(1, d))


def _in_proj_kernel(layer, x_ref, wt_hbm, o_ref, stage_ref, wb_ref, sem):
    j = pl.program_id(0)
    i = pl.program_id(1)

    def w_copy(tile, slot):
        row0 = jnp.where(tile < N_QKV_TILES, tile * TN_PROJ, G_COL0 + (tile - N_QKV_TILES) * TN_PROJ)
        row0 = pl.multiple_of(row0, 8)
        return pltpu.make_async_copy(wt_hbm.at[layer, pl.ds(row0, TN_PROJ), :],
                                     stage_ref.at[slot], sem.at[slot])

    @pl.when(i == 0)
    def _():
        @pl.when(j == 0)
        def _():
            w_copy(0, 0).start()

        @pl.when(j + 1 < pl.num_programs(0))
        def _():
            w_copy(j + 1, (j + 1) % 2).start()

        w_copy(j, j % 2).wait()
        wb_ref[...] = stage_ref[j % 2].astype(BF16)

    acc = lax.dot_general(x_ref[...], wb_ref[...], (((1,), (1,)), ((), ())),
                          preferred_element_type=F32)

    @pl.when(j < N_QKV_TILES)
    def _():
        part = j % (3 * WIDTH // TN_PROJ)
        is_q = part < WIDTH // TN_PROJ
        is_v = part >= 2 * WIDTH // TN_PROJ
        o_ref[...] = (acc * jnp.where(is_q, QK_SCALE, 1.0)).astype(BF16)

        @pl.when(jnp.logical_and(is_v, i == 0))
        def _():
            o_ref[0:N_PAD, :] = jnp.zeros((N_PAD, TN_PROJ), BF16)

    @pl.when(j >= N_QKV_TILES)
    def _():
        o_ref[...] = jax.nn.sigmoid(acc).astype(BF16)


def _in_proj(hb, w_in_t, layer):
    l, d = hb.shape
    n_tiles = N_QKV_TILES + 2 * D_MODEL // TN_PROJ
    return pl.pallas_call(
        functools.partial(_in_proj_kernel, layer),
        grid=(n_tiles, l // TM_PROJ),
        in_specs=[
            pl.BlockSpec((TM_PROJ, d), lambda j, i: (i, 0)),
            pl.BlockSpec(memory_space=pl.ANY),
        ],
        out_specs=pl.BlockSpec((TM_PROJ, TN_PROJ), lambda j, i: (i, j)),
        out_shape=jax.ShapeDtypeStruct((l, n_tiles * TN_PROJ), BF16),
        scratch_shapes=[pltpu.VMEM((2, TN_PROJ, d), F32), pltpu.VMEM((TN_PROJ, d), BF16),
                        pltpu.SemaphoreType.DMA((2,))],
        compiler_params=_cparams(("arbitrary", "arbitrary")),
        name="in_proj",
    )(hb, w_in_t)


def _forget_kernel(h_ref, w_ref, b_ref, o_ref, carry_ref):
    i = pl.program_id(0)

    @pl.when(i == 0)
    def _():
        carry_ref[...] = jnp.zeros_like(carry_ref)

    f = lax.dot_general(w_ref[...].astype(BF16), h_ref[...], (((1,), (1,)), ((), ())),
                        preferred_element_type=F32) + b_ref[...]
    pos = i * TM_LN + lax.broadcasted_iota(jnp.int32, f.shape, 1)
    lf = jnp.where(pos >= N_PAD, _log_sigmoid(f), 0.0)
    r = lax.broadcasted_iota(jnp.int32, (TM_LN, TM_LN), 0)
    c = lax.broadcasted_iota(jnp.int32, (TM_LN, TM_LN), 1)
    upto = jnp.where(r <= c, 1.0, 0.0).astype(BF16)
    hi = lf.astype(BF16)
    rest = lf - hi.astype(F32)
    mid = rest.astype(BF16)
    lo = (rest - mid.astype(F32)).astype(BF16)
    prefix = lambda part: jnp.dot(part, upto, preferred_element_type=F32)
    cs = (prefix(lo) + prefix(mid)) + prefix(hi) + carry_ref[:, 0:1]
    o_ref[...] = jnp.where(pos >= N_PAD, -cs, NEG)
    carry_ref[...] = jnp.broadcast_to(cs[:, TM_LN - 1:TM_LN], carry_ref.shape)


def _forget_bias(hb, w_in_t, bf, layer):
    l, d = hb.shape
    return pl.pallas_call(
        _forget_kernel,
        grid=(l // TM_LN,),
        in_specs=[
            pl.BlockSpec((TM_LN, d), lambda i: (i, 0)),
            pl.BlockSpec((None, N_HEADS, d), lambda i: (layer, F_COL0 // N_HEADS, 0)),
            pl.BlockSpec((N_HEADS, TM_LN), lambda i: (0, 0)),
        ],
        out_specs=pl.BlockSpec((N_HEADS, TM_LN), lambda i: (0, i)),
        out_shape=jax.ShapeDtypeStruct((N_HEADS, l), F32),
        scratch_shapes=[pltpu.VMEM((N_HEADS, LANE), F32)],
        compiler_params=_cparams(("arbitrary",)),
        name="forget_bias",
    )(hb, w_in_t, bf)


def _sb_kernel(q_ref, k_ref, v_ref, o_ref, run_ref, acc_ref):
    i = pl.program_id(0)
    tq, tk = SB_TQ, SB_TK
    r = lax.broadcasted_iota(jnp.int32, (2 * tk, 2 * tk), 0)
    c = lax.broadcasted_iota(jnp.int32, (2 * tk, 2 * tk), 1)
    suffix = jnp.where(jnp.logical_or(c >= tk, (r % tk) > c), 1.0, 0.0).astype(BF16)
    row = lax.broadcasted_iota(jnp.int32, (tq, tk), 0)
    col = lax.broadcasted_iota(jnp.int32, (tq, tk), 1)
    strictly_causal = col < row

    def block(j, diagonal):
        start = pl.multiple_of(j * tk, tk)
        heads = range(N_HEADS)
        cols = [slice(hd * HEAD_DIM, (hd + 1) * HEAD_DIM) for hd in heads]
        zs = [lax.dot_general(q_ref[:, c], k_ref[pl.ds(start, tk), c], (((1,), (1,)), ((), ())),
                              preferred_element_type=F32) for c in cols]
        log_betas, cats = [], []
        for z in zs:
            log_beta = _log_sigmoid(z)
            log_keep = log_beta - z
            if diagonal:
                log_keep = jnp.where(strictly_causal, log_keep, 0.0)
            hi = log_keep.astype(BF16)
            lo = (log_keep - hi.astype(F32)).astype(BF16)
            log_betas.append(log_beta)
            cats.append(jnp.concatenate([hi, lo], axis=1))
        sums = [jnp.dot(cat, suffix, preferred_element_type=F32) for cat in cats]
        ws, worst = [], None
        for c, log_beta, s in zip(cols, log_betas, sums):
            if diagonal:
                w = jnp.where(strictly_causal, jnp.exp(log_beta + s[:, :tk]), 0.0)
                run = s[:, tk:]
            else:
                run = run_ref[:, c]
                w = jnp.exp(log_beta + s[:, :tk] + run)
                run = run + s[:, tk:]
            run_ref[:, c] = run
            ws.append(w.astype(BF16))
            worst = run if worst is None else jnp.maximum(worst, run)
        for c, w in zip(cols, ws):
            pv = jnp.dot(w, v_ref[pl.ds(start, tk), c], preferred_element_type=F32)
            if diagonal:
                acc_ref[:, c] = pv
            else:
                acc_ref[:, c] += pv
        return jnp.max(worst)

    worst = block(i, True)

    def cond(s):
        j, worst = s
        return jnp.logical_and(j >= 0, worst > SB_EXIT)

    def body(s):
        j, _ = s
        return j - 1, block(j, False)

    lax.while_loop(cond, body, (i - 1, worst))
    o_ref[...] = acc_ref[...].astype(BF16)


def _sb_attention(proj):
    l = proj.shape[0]
    resident = lambda blk: pl.BlockSpec((l, WIDTH), lambda i: (0, blk), pipeline_mode=pl.Buffered(1))
    return pl.pallas_call(
        _sb_kernel,
        grid=(l // SB_TQ,),
        in_specs=[pl.BlockSpec((SB_TQ, WIDTH), lambda i: (i, 0)), resident(1), resident(2)],
        out_specs=pl.BlockSpec((SB_TQ, WIDTH), lambda i: (i, 0)),
        out_shape=jax.ShapeDtypeStruct((l, WIDTH), BF16),
        scratch_shapes=[pltpu.VMEM((SB_TQ, WIDTH), F32), pltpu.VMEM((SB_TQ, WIDTH), F32)],
        compiler_params=_cparams(("arbitrary",)),
        name="sb_attention",
    )(proj, proj, proj)


def _fox_kernel(q_ref, k_ref, v_ref, b_ref, o_ref, vaug_ref):
    i = pl.program_id(1)
    t = FOX_T

    @pl.when(i == 0)
    def _():
        lane = lax.broadcasted_iota(jnp.int32, (vaug_ref.shape[1], HEAD_DIM), 1)
        ones_col = jnp.where(lane == 0, 1.0, 0.0).astype(BF16)
        for hd in range(FOX_GROUP):
            vaug_ref[hd, :, :HEAD_DIM] = v_ref[:, hd * HEAD_DIM:(hd + 1) * HEAD_DIM]
            vaug_ref[hd, :, HEAD_DIM:] = ones_col

    row = lax.broadcasted_iota(jnp.int32, (t, t), 0)
    col = lax.broadcasted_iota(jnp.int32, (t, t), 1)
    causal = col <= row

    def block(j, state, diagonal):
        start = pl.multiple_of(j * t, t)
        heads = range(FOX_GROUP)
        cols = [slice(hd * HEAD_DIM, (hd + 1) * HEAD_DIM) for hd in heads]
        scores = [lax.dot_general(q_ref[:, c], k_ref[pl.ds(start, t), c], (((1,), (1,)), ((), ())),
                                  preferred_element_type=F32) for c in cols]
        ps, scales, ms = [], [], []
        for hd, s in zip(heads, scores):
            m = state[hd][0]
            s = s + b_ref[hd, j]
            if diagonal:
                s = jnp.where(causal, s, NEG)
            m_new = jnp.maximum(m, jnp.max(s, axis=1, keepdims=True))
            ps.append(jnp.exp(s - m_new).astype(BF16))
            scales.append(jnp.exp(m - m_new))
            ms.append(m_new)
        out = []
        for hd in heads:
            pv = jnp.dot(ps[hd], vaug_ref[hd, pl.ds(start, t), :], preferred_element_type=F32)
            out.append((ms[hd], scales[hd] * state[hd][1] + pv))
        return tuple(out)

    init = tuple((jnp.full((t, 1), NEG, F32), jnp.zeros((t, 2 * HEAD_DIM), F32))
                 for _ in range(FOX_GROUP))
    state = block(i, init, True)
    state = lax.fori_loop(0, i, lambda j, s: block(j, s, False), state)
    for hd, (_, acc) in enumerate(state):
        o_ref[:, hd * HEAD_DIM:(hd + 1) * HEAD_DIM] = (
            acc[:, :HEAD_DIM] / acc[:, HEAD_DIM:HEAD_DIM + 1]).astype(BF16)


def _fox_attention(proj, key_bias):
    l = proj.shape[0]
    gw = FOX_GROUP * HEAD_DIM
    first = 3 * WIDTH // gw
    per = WIDTH // gw
    nkb = l // FOX_T
    return pl.pallas_call(
        _fox_kernel,
        grid=(N_HEADS // FOX_GROUP, l // FOX_T),
        in_specs=[
            pl.BlockSpec((FOX_T, gw), lambda g, i: (i, first + g)),
            pl.BlockSpec((l, gw), lambda g, i: (0, first + per + g), pipeline_mode=pl.Buffered(1)),
            pl.BlockSpec((l, gw), lambda g, i: (0, first + 2 * per + g), pipeline_mode=pl.Buffered(1)),
            pl.BlockSpec((FOX_GROUP, nkb, 1, FOX_T), lambda g, i: (g, 0, 0, 0)),
        ],
        out_specs=pl.BlockSpec((FOX_T, gw), lambda g, i: (i, g)),
        out_shape=jax.ShapeDtypeStruct((l, WIDTH), BF16),
        scratch_shapes=[pltpu.VMEM((FOX_GROUP, l, 2 * HEAD_DIM), BF16)],
        compiler_params=_cparams(("arbitrary", "arbitrary")),
        name="fox_attention",
    )(proj, proj, proj, key_bias)


def _merge_kernel(ys_ref, yf_ref, gs_ref, gf_ref, ws_ref, wf_ref, o_ref, wsb_ref, wfb_ref):
    i = pl.program_id(1)

    @pl.when(i == 0)
    def _():
        wsb_ref[...] = ws_ref[...].astype(BF16)
        wfb_ref[...] = wf_ref[...].astype(BF16)

    a = jnp.dot(ys_ref[...], wsb_ref[...], preferred_element_type=F32)
    b = jnp.dot(yf_ref[...], wfb_ref[...], preferred_element_type=F32)
    o_ref[...] = (gs_ref[...].astype(F32) * a + gf_ref[...].astype(F32) * b).astype(BF16)


def _merge(y_sb, y_fx, proj, w_bsb, w_bfx, layer):
    l = y_sb.shape[0]
    d = D_MODEL
    nj = d // TN_PROJ
    act = pl.BlockSpec((TM_PROJ, WIDTH), lambda j, i: (i, 0))
    wspec = pl.BlockSpec((None, WIDTH, TN_PROJ), lambda j, i: (layer, 0, j))
    return pl.pallas_call(
        _merge_kernel,
        grid=(nj, l // TM_PROJ),
        in_specs=[
            act, act,
            pl.BlockSpec((TM_PROJ, TN_PROJ), lambda j, i: (i, N_QKV_TILES + j)),
            pl.BlockSpec((TM_PROJ, TN_PROJ), lambda j, i: (i, N_QKV_TILES + nj + j)),
            wspec, wspec,
        ],
        out_specs=pl.BlockSpec((TM_PROJ, TN_PROJ), lambda j, i: (i, j)),
        out_shape=jax.ShapeDtypeStruct((l, d), BF16),
        scratch_shapes=[pltpu.VMEM((WIDTH, TN_PROJ), BF16), pltpu.VMEM((WIDTH, TN_PROJ), BF16)],
        compiler_params=_cparams(("arbitrary", "arbitrary")),
        name="merge",
    )(y_sb, y_fx, proj, proj, w_bsb, w_bfx)


def _pick(index, values):
    out = values[0]
    for k in range(1, len(values)):
        out = jnp.where(index == k, values[k], out)
    return out


def _grouped_top2(sel, aff):
    srow = [sel[e:e + 1, :] for e in range(N_EXPERTS)]
    arow = [aff[e:e + 1, :] for e in range(N_EXPERTS)]
    n = EXPERTS_PER_GROUP
    score = []
    for g in range(N_GROUPS):
        v = srow[g * n:(g + 1) * n]
        pair = [v[a] + v[b] for a in range(n) for b in range(a + 1, n)]
        score.append(functools.reduce(jnp.maximum, pair))
    group = jnp.zeros_like(score[0], dtype=jnp.int32)
    best = score[0]
    for g in range(1, N_GROUPS):
        better = score[g] > best
        group = jnp.where(better, g, group)
        best = jnp.where(better, score[g], best)
    v = [_pick(group, [srow[g * n + k] for g in range(N_GROUPS)]) for k in range(n)]
    a = [_pick(group, [arow[g * n + k] for g in range(N_GROUPS)]) for k in range(n)]
    i1 = jnp.zeros_like(group)
    m1 = v[0]
    for k in range(1, n):
        better = v[k] > m1
        i1 = jnp.where(better, k, i1)
        m1 = jnp.where(better, v[k], m1)
    first_is_0 = i1 == 0
    i2 = jnp.where(first_is_0, 1, 0)
    m2 = jnp.where(first_is_0, v[1], v[0])
    for k in range(1, n):
        better = jnp.logical_and(i1 != k, v[k] > m2)
        i2 = jnp.where(better, k, i2)
        m2 = jnp.where(better, v[k], m2)
    a1 = _pick(i1, a)
    a2 = _pick(i2, a)
    tot = a1 + a2
    return group * n + i1, group * n + i2, a1 / tot, a2 / tot


def _out_ln_kernel(layer, x_ref, w_hbm, h_ref, g_ref, b_ref, wr_ref, rb_ref,
                   of_ref, ob_ref, hp_ref, ri_ref, rw_ref, cnt_ref, stage_ref, wb_ref, sem):
    i = pl.program_id(0)

    @pl.when(i == 0)
    def _():
        cnt_ref[...] = jnp.zeros_like(cnt_ref)
        n_chunks = D_MODEL // W_CHUNK

        def w_copy(c):
            return pltpu.make_async_copy(w_hbm.at[layer, pl.ds(c * W_CHUNK, W_CHUNK), :],
                                         stage_ref.at[c % 2], sem.at[c % 2])

        w_copy(0).start()
        for c in range(n_chunks):
            if c + 1 < n_chunks:
                w_copy(c + 1).start()
            w_copy(c).wait()
            wb_ref[c * W_CHUNK:(c + 1) * W_CHUNK, :] = stage_ref[c % 2].astype(BF16)

    mix = jnp.dot(x_ref[...], wb_ref[...], preferred_element_type=F32)

    y = _layer_norm(ALPHA * h_ref[...] + mix, g_ref[...], b_ref[...])
    of_ref[...] = y
    ob_ref[...] = y.astype(BF16)
    hp_ref[...] = _pack_bf16_pairs(y)

    logits = lax.dot_general(wr_ref[...], y, (((1,), (1,)), ((), ())),
                             precision=lax.Precision.HIGHEST, preferred_element_type=F32)
    aff = jax.nn.sigmoid(logits)
    e1, e2, w1, w2 = _grouped_top2(aff + rb_ref[...], aff)
    t = TM_LN
    eid = lax.broadcasted_iota(jnp.int32, (N_EXPERTS, t), 0)
    hit1 = eid == e1
    hit2 = eid == e2
    onehot = jnp.where(jnp.logical_or(hit1, hit2), 1.0, 0.0)
    r = lax.broadcasted_iota(jnp.int32, (t, t), 0)
    c = lax.broadcasted_iota(jnp.int32, (t, t), 1)
    before = jnp.where(r < c, 1.0, 0.0).astype(BF16)
    seen = jnp.dot(onehot.astype(BF16), before, preferred_element_type=F32) + cnt_ref[:, 0:1]
    rank1 = jnp.sum(jnp.where(hit1, seen, 0.0), axis=0, keepdims=True)
    rank2 = jnp.sum(jnp.where(hit2, seen, 0.0), axis=0, keepdims=True)
    ri_ref[0:1, :] = e1
    ri_ref[1:2, :] = e2
    ri_ref[2:3, :] = rank1.astype(jnp.int32)
    ri_ref[3:4, :] = rank2.astype(jnp.int32)
    ri_ref[4:8, :] = jnp.zeros((4, t), jnp.int32)
    rw_ref[0:1, :] = w1
    rw_ref[1:2, :] = w2
    rw_ref[2:8, :] = jnp.zeros((6, t), F32)
    cnt_ref[...] += jnp.sum(onehot, axis=1, keepdims=True)


def _out_ln(merged, w_out, hf, g, b, wr_t, rb, layer):
    l, d = hf.shape
    row = pl.BlockSpec((TM_LN, d), lambda i: (i, 0))
    vec = pl.BlockSpec((None, 1, d), lambda i: (layer, 0, 0))
    tok = pl.BlockSpec((8, TM_LN), lambda i: (0, i))
    return pl.pallas_call(
        functools.partial(_out_ln_kernel, layer),
        grid=(l // TM_LN,),
        in_specs=[
            row,
            pl.BlockSpec(memory_space=pl.ANY),
            row, vec, vec,
            pl.BlockSpec((N_EXPERTS, d), lambda i: (0, 0)),
            pl.BlockSpec((N_EXPERTS, TM_LN), lambda i: (0, 0)),
        ],
        out_specs=[row, row, pl.BlockSpec((TM_LN, d // 2), lambda i: (i, 0)), tok, tok,
                   pl.BlockSpec((N_EXPERTS, LANE), lambda i: (0, 0))],
        out_shape=[jax.ShapeDtypeStruct((l, d), F32), jax.ShapeDtypeStruct((l, d), BF16),
                   jax.ShapeDtypeStruct((l, d // 2), jnp.uint32),
                   jax.ShapeDtypeStruct((8, l), jnp.int32), jax.ShapeDtypeStruct((8, l), F32),
                   jax.ShapeDtypeStruct((N_EXPERTS, LANE), F32)],
        scratch_shapes=[pltpu.VMEM((2, W_CHUNK, d), F32), pltpu.VMEM((d, d), BF16),
                        pltpu.SemaphoreType.DMA((2,))],
        compiler_params=_cparams(("arbitrary",)),
        name="out_ln",
    )(merged, w_out, hf, g.reshape(DEPTH, 1, d), b.reshape(DEPTH, 1, d), wr_t, rb)


def _invert_kernel(pos1_ref, pos2_ref, tok_ref):
    def clear(r, carry):
        tok_ref[r] = 0
        return carry

    lax.fori_loop(0, tok_ref.shape[0], clear, 0, unroll=8)

    def mark(t, carry):
        tok_ref[pos1_ref[t]] = t
        tok_ref[pos2_ref[t]] = t
        return carry

    lax.fori_loop(0, pos1_ref.shape[0], mark, 0, unroll=8)


def _invert(pos1, pos2, n_rows):
    smem = pl.BlockSpec(memory_space=pltpu.SMEM)
    return pl.pallas_call(
        _invert_kernel,
        in_specs=[smem, smem],
        out_specs=smem,
        out_shape=jax.ShapeDtypeStruct((n_rows,), jnp.int32),
        name="invert",
    )(pos1, pos2)


def _pack_bf16_pairs(y):
    bits = pltpu.bitcast(y.astype(BF16).astype(F32), jnp.uint32)
    half = y.shape[1] // 2
    return (bits[:, :half] >> 16) | (bits[:, half:] & jnp.uint32(0xFFFF0000))


def _unpack_bf16_pairs(words):
    return (pltpu.bitcast(words << 16, F32), pltpu.bitcast(words & jnp.uint32(0xFFFF0000), F32))


def _expert_kernel(be_ref, nu_ref, nv_ref, tok_ref, hp_hbm, wg_ref, wu_ref, wd_ref, o_ref,
                   xs_ref, xb_ref, acc_ref, wgb_ref, wub_ref, wdb_ref, sem):
    blk = pl.program_id(0)
    f = pl.program_id(1)
    nf = EXPERT_FF // MOE_TF
    used = blk < nu_ref[0]
    slot = blk % 2
    per_step = MOE_CHUNK // nf
    chunks = [slice(c * MOE_CHUNK, (c + 1) * MOE_CHUNK) for c in range(MOE_CHUNKS)]
    here = [nv_ref[blk] > c * MOE_CHUNK for c in range(MOE_CHUNKS)]
    ahead = [nv_ref[blk + 1] > c * MOE_CHUNK for c in range(MOE_CHUNKS)]

    def row_copy(s, r, src_row):
        return pltpu.make_async_copy(hp_hbm.at[pl.ds(src_row, 1)], xs_ref.at[s, pl.ds(r, 1)], sem.at[s])

    def issue_chunk(first_row):
        def issue(r, carry):
            row_copy(0, first_row + r, tok_ref[first_row + r]).start()
            return carry

        lax.fori_loop(0, MOE_CHUNK, issue, 0, unroll=8)

    def wait_chunk():
        def wait(r, carry):
            row_copy(slot, 0, 0).wait()
            return carry

        lax.fori_loop(0, MOE_CHUNK, wait, 0, unroll=8)

    def unpack(rows):
        lo, hi = _unpack_bf16_pairs(xs_ref[slot, rows, :])
        xb_ref[rows, :D_MODEL // 2] = lo.astype(BF16)
        xb_ref[rows, D_MODEL // 2:] = hi.astype(BF16)

    def compute(rows):
        x = xb_ref[rows, :]
        g = jnp.dot(x, wgb_ref[...], preferred_element_type=F32)
        u = jnp.dot(x, wub_ref[...], preferred_element_type=F32)
        a = (g * jax.nn.sigmoid(g) * u).astype(BF16)
        acc_ref[rows, :] += jnp.dot(a, wdb_ref[...], preferred_element_type=F32)

    for c in range(MOE_CHUNKS):
        @pl.when(jnp.logical_and(jnp.logical_and(blk == 0, f == 0), here[c]))
        def _():
            issue_chunk(c * MOE_CHUNK)

    for c in range(MOE_CHUNKS):
        @pl.when(jnp.logical_and(f == 0, here[c]))
        def _():
            wait_chunk()

    for c in range(MOE_CHUNKS):
        @pl.when(jnp.logical_and(f == 0, here[c]))
        def _():
            unpack(chunks[c])

    @pl.when(jnp.logical_and(used, f == 0))
    def _():
        acc_ref[...] = jnp.zeros_like(acc_ref)

    @pl.when(used)
    def _():
        wgb_ref[...] = wg_ref[...].astype(BF16)
        wub_ref[...] = wu_ref[...].astype(BF16)
        wdb_ref[...] = wd_ref[...].astype(BF16)

    for c in range(MOE_CHUNKS):
        @pl.when(jnp.logical_and(used, ahead[c]))
        def _():
            nxt = (blk + 1) * MOE_TM
            for r in range(per_step):
                row = c * MOE_CHUNK + f * per_step + r
                row_copy(1 - slot, row, tok_ref[nxt + row]).start()

        @pl.when(here[c])
        def _():
            compute(chunks[c])

    @pl.when(jnp.logical_and(f == nf - 1, used))
    def _():
        o_ref[...] = _pack_bf16_pairs(acc_ref[...])

    @pl.when(jnp.logical_and(f == nf - 1, jnp.logical_not(used)))
    def _():
        o_ref[...] = jnp.zeros_like(o_ref)


def _expert_ffn(block_e, n_used, n_valid, row_tok, hp, w_gate, w_up, w_down, layer):
    d = D_MODEL
    nb = block_e.shape[0]
    nf = EXPERT_FF // MOE_TF

    def w_in_map(b, f, be, nu, nv, tok):
        return (layer, be[b], 0, jnp.where(b < nu[0], f, nf - 1))

    def w_out_map(b, f, be, nu, nv, tok):
        return (layer, be[b], jnp.where(b < nu[0], f, nf - 1), 0)

    grid_spec = pltpu.PrefetchScalarGridSpec(
        num_scalar_prefetch=4,
        grid=(nb, nf),
        in_specs=[
            pl.BlockSpec(memory_space=pl.ANY),
            pl.BlockSpec((None, None, d, MOE_TF), w_in_map),
            pl.BlockSpec((None, None, d, MOE_TF), w_in_map),
            pl.BlockSpec((None, None, MOE_TF, d), w_out_map),
        ],
        out_specs=pl.BlockSpec((MOE_TM, d // 2), lambda b, f, be, nu, nv, tok: (b, 0)),
        scratch_shapes=[pltpu.VMEM((2, MOE_TM, d // 2), jnp.uint32), pltpu.VMEM((MOE_TM, d), BF16),
                        pltpu.VMEM((MOE_TM, d), F32), pltpu.VMEM((d, MOE_TF), BF16),
                        pltpu.VMEM((d, MOE_TF), BF16), pltpu.VMEM((MOE_TF, d), BF16),
                        pltpu.SemaphoreType.DMA((2,))],
    )
    return pl.pallas_call(
        _expert_kernel,
        grid_spec=grid_spec,
        out_shape=jax.ShapeDtypeStruct((nb * MOE_TM, d // 2), jnp.uint32),
        compiler_params=_cparams(("arbitrary", "arbitrary")),
        name="expert_ffn",
    )(block_e, n_used, n_valid, row_tok, hp, w_gate, w_up, w_down)


def _combine_kernel(first_row, with_bf16, pos1_ref, pos2_ref, yb_hbm, h_ref, w_ref, g_ref, b_ref, *rest):
    out_refs, (y1_ref, y2_ref, sem) = rest[:-3], rest[-3:]
    i = pl.program_id(0)
    slot = i % 2

    def row_copy(dst, s, r, src_row):
        return pltpu.make_async_copy(yb_hbm.at[pl.ds(src_row, 1)], dst.at[s, pl.ds(r, 1)], sem.at[s])

    def issue_tile(tile, s):
        base = first_row + tile * CMB_TM

        def issue(r, carry):
            row_copy(y1_ref, s, r, pos1_ref[base + r]).start()
            row_copy(y2_ref, s, r, pos2_ref[base + r]).start()
            return carry

        lax.fori_loop(0, CMB_TM, issue, 0, unroll=8)

    @pl.when(i == 0)
    def _():
        issue_tile(0, 0)

    @pl.when(i + 1 < pl.num_programs(0))
    def _():
        issue_tile(i + 1, 1 - slot)

    def wait(r, carry):
        row_copy(y1_ref, slot, 0, 0).wait()
        row_copy(y2_ref, slot, 0, 0).wait()
        return carry

    lax.fori_loop(0, CMB_TM, wait, 0, unroll=8)
    w = w_ref[...]
    lo1, hi1 = _unpack_bf16_pairs(y1_ref[slot])
    lo2, hi2 = _unpack_bf16_pairs(y2_ref[slot])
    y = jnp.concatenate([w[:, 0:1] * lo1 + w[:, 1:2] * lo2, w[:, 0:1] * hi1 + w[:, 1:2] * hi2], axis=1)
    out = _layer_norm(ALPHA * h_ref[...] + y, g_ref[...], b_ref[...])
    out_refs[0][...] = out
    if with_bf16:
        out_refs[1][...] = out.astype(BF16)


def _combine_ln(pos1, pos2, yb, hf, wts, g, b, layer, final):
    l, d = hf.shape
    first_block = PREFIX // CMB_TM if final else 0
    n_rows = l - first_block * CMB_TM
    src = lambda i, p1, p2: (i + first_block, 0)
    dst = lambda i, p1, p2: (i, 0)
    vec = pl.BlockSpec((None, 1, d), lambda i, p1, p2: (layer, 0, 0))
    out_specs = [pl.BlockSpec((CMB_TM, d), dst)]
    out_shape = [jax.ShapeDtypeStruct((n_rows, d), F32)]
    if not final:
        out_specs.append(pl.BlockSpec((CMB_TM, d), dst))
        out_shape.append(jax.ShapeDtypeStruct((n_rows, d), BF16))
    grid_spec = pltpu.PrefetchScalarGridSpec(
        num_scalar_prefetch=2,
        grid=(n_rows // CMB_TM,),
        in_specs=[
            pl.BlockSpec(memory_space=pl.ANY),
            pl.BlockSpec((CMB_TM, d), src),
            pl.BlockSpec((CMB_TM, TOP_K), src),
            vec, vec,
        ],
        out_specs=out_specs,
        scratch_shapes=[
            pltpu.VMEM((2, CMB_TM, d // 2), jnp.uint32),
            pltpu.VMEM((2, CMB_TM, d // 2), jnp.uint32),
            pltpu.SemaphoreType.DMA((2,)),
        ],
    )
    return pl.pallas_call(
        functools.partial(_combine_kernel, first_block * CMB_TM, not final),
        grid_spec=grid_spec,
        out_shape=out_shape,
        compiler_params=_cparams(("arbitrary",)),
        name="combine_ln",
    )(pos1, pos2, yb, hf, wts, g.reshape(DEPTH, 1, d), b.reshape(DEPTH, 1, d))


def _plan(route_i, counts_f):
    n = route_i.shape[1]
    nb = -(-(n * TOP_K + N_EXPERTS * (MOE_TM - 1)) // MOE_TM)
    counts = counts_f[:, 0].astype(jnp.int32)
    blocks_e = (counts + MOE_TM - 1) // MOE_TM
    bend = jnp.cumsum(blocks_e)
    base = (bend - blocks_e) * MOE_TM
    eids = jnp.arange(N_EXPERTS, dtype=jnp.int32)[:, None]
    base_of = lambda e: jnp.sum(jnp.where(e[None, :] == eids, base[:, None], 0), axis=0)
    pos1 = (route_i[2] + base_of(route_i[0])).astype(jnp.int32)
    pos2 = (route_i[3] + base_of(route_i[1])).astype(jnp.int32)
    n_used = bend[-1:].astype(jnp.int32)
    blocks = jnp.arange(nb, dtype=jnp.int32)
    block_e = jnp.minimum(jnp.searchsorted(bend, blocks, side='right'), N_EXPERTS - 1).astype(jnp.int32)
    done = (blocks - (bend - blocks_e)[block_e]) * MOE_TM
    n_valid = jnp.where(blocks < n_used[0], jnp.clip(counts[block_e] - done, 0, MOE_TM), 0)
    n_valid = jnp.concatenate([n_valid, jnp.zeros((1,), jnp.int32)]).astype(jnp.int32)
    return pos1, pos2, block_e, n_used, n_valid


def kernel(x, meta_tokens, ln_in_g, ln_in_b, w_in, b_forget, w_branch_sb, w_branch_fox, w_out,
           ln_mix_g, ln_mix_b, w_router, router_bias, w_gate, w_up, w_down, ln_ffn_g, ln_ffn_b):
    b, s, d = x.shape
    assert b == 1 and d == D_MODEL
    l = s + PREFIX
    prefix = jnp.concatenate([jnp.zeros((N_PAD, d), x.dtype), meta_tokens.astype(x.dtype)], axis=0)
    hf, hb = _ln_in(prefix, x[0], ln_in_g, ln_in_b)

    w_in_t = jnp.swapaxes(w_in, 1, 2)
    wr_t = w_router.astype(F32).T
    rb = jnp.broadcast_to(router_bias.astype(F32)[:, None], (N_EXPERTS, TM_LN))
    for layer in range(DEPTH):
        proj = _in_proj(hb, w_in_t, layer)
        bf = jnp.broadcast_to(b_forget[layer].astype(F32)[:, None], (N_HEADS, TM_LN))
        key_bias = _forget_bias(hb, w_in_t, bf, layer).reshape(N_HEADS, l // FOX_T, 1, FOX_T)
        y_sb = _sb_attention(proj)
        y_fx = _fox_attention(proj, key_bias)
        merged = _merge(y_sb, y_fx, proj, w_branch_sb, w_branch_fox, layer)
        hf, hb, hp, route_i, route_w, counts = _out_ln(merged, w_out, hf, ln_mix_g, ln_mix_b,
                                                       wr_t, rb, layer)

        pos1, pos2, block_e, n_used, n_valid = _plan(route_i, counts)
        row_tok = _invert(pos1, pos2, block_e.shape[0] * MOE_TM)
        yb = _expert_ffn(block_e, n_used, n_valid, row_tok, hp, w_gate, w_up, w_down, layer)
        final = layer == DEPTH - 1
        outs = _combine_ln(pos1, pos2, yb, hf, route_w[:TOP_K].T, ln_ffn_g, ln_ffn_b, layer, final)
        if not final:
            hf, hb = outs
    return outs[0][None]
```

```python
import functools

import jax
import jax.numpy as jnp
from jax import lax
from jax.experimental import pallas as pl
from jax.experimental.pallas import tpu as pltpu

F32 = jnp.float32
BF16 = jnp.bfloat16

D_MODEL = 2048
DEPTH = 2
HEAD_DIM = 128
N_HEADS = 8
WIDTH = N_HEADS * HEAD_DIM
N_META = 16
N_PAD = 112
PREFIX = N_PAD + N_META
N_EXPERTS = 16
N_GROUPS = 4
EXPERTS_PER_GROUP = N_EXPERTS // N_GROUPS
TOP_K = 2
EXPERT_FF = 1024
ALPHA = (2 * DEPTH) ** 0.25
LN_EPS = 1e-5
NEG = -1e30
QK_SCALE = HEAD_DIM ** -0.5
QKV_COLS = 6 * WIDTH
F_COL0 = QKV_COLS
G_COL0 = QKV_COLS + N_HEADS

SB_EXIT = -104.0

LANE = 128
VMEM_LIMIT = 56 * 1024 * 1024
TM_PROJ = 1664
TN_PROJ = 512
N_QKV_TILES = QKV_COLS // TN_PROJ
TM_LN = 640
W_CHUNK = 256
SB_TQ = 128
SB_TK = 128
FOX_T = 640
FOX_GROUP = 4
MOE_CHUNK = 384
MOE_CHUNKS = 3
MOE_TM = MOE_CHUNK * MOE_CHUNKS
MOE_TF = 256
CMB_TM = 128


def _cparams(sem):
    return pltpu.CompilerParams(dimension_semantics=sem, vmem_limit_bytes=VMEM_LIMIT)


def _layer_norm(x, g, b):
    mu = jnp.mean(x, axis=-1, keepdims=True)
    xc = x - mu
    var = jnp.mean(xc * xc, axis=-1, keepdims=True)
    return xc * lax.rsqrt(var + LN_EPS) * g + b


def _log_sigmoid(z):
    return jnp.minimum(z, 0.0) - jnp.log(1.0 + jnp.exp(-jnp.abs(z)))


def _ln_in_kernel(p_ref, x_ref, g_ref, b_ref, of_ref, ob_ref):
    src = jnp.where(pl.program_id(0) == 0, p_ref[...], x_ref[...])
    y = _layer_norm(src, g_ref[...], b_ref[...])
    of_ref[...] = y
    ob_ref[...] = y.astype(BF16)


def _ln_in(prefix, x, g, b):
    s, d = x.shape
    l = s + PREFIX
    row = pl.BlockSpec((PREFIX, d), lambda i: (i, 0))
    vec = pl.BlockSpec((1, d), lambda i: (0, 0))
    return pl.pallas_call(
        _ln_in_kernel,
        grid=(l // PREFIX,),
        in_specs=[
            pl.BlockSpec((PREFIX, d), lambda i: (0, 0)),
            pl.BlockSpec((PREFIX, d), lambda i: (jnp.maximum(i - 1, 0), 0)),
            vec, vec,
        ],
        out_specs=[row, row],
        out_shape=[jax.ShapeDtypeStruct((l, d), F32), jax.ShapeDtypeStruct((l, d), BF16)],
        compiler_params=_cparams(("arbitrary",)),
        name="ln_in",
    )(prefix, x, g.reshape(1, d), b.reshape(1, d))


def _in_proj_kernel(layer, x_ref, wt_hbm, o_ref, stage_ref, wb_ref, sem):
    j = pl.program_id(0)
    i = pl.program_id(1)

    def w_copy(tile, slot):
        row0 = jnp.where(tile < N_QKV_TILES, tile * TN_PROJ, G_COL0 + (tile - N_QKV_TILES) * TN_PROJ)
        row0 = pl.multiple_of(row0, 8)
        return pltpu.make_async_copy(wt_hbm.at[layer, pl.ds(row0, TN_PROJ), :],
                                     stage_ref.at[slot], sem.at[slot])

    @pl.when(i == 0)
    def _():
        @pl.when(j == 0)
        def _():
            w_copy(0, 0).start()

        @pl.when(j + 1 < pl.num_programs(0))
        def _():
            w_copy(j + 1, (j + 1) % 2).start()

        w_copy(j, j % 2).wait()
        wb_ref[...] = stage_ref[j % 2].astype(BF16)

    acc = lax.dot_general(x_ref[...], wb_ref[...], (((1,), (1,)), ((), ())),
                          preferred_element_type=F32)

    @pl.when(j < N_QKV_TILES)
    def _():
        part = j % (3 * WIDTH // TN_PROJ)
        is_q = part < WIDTH // TN_PROJ
        is_v = part >= 2 * WIDTH // TN_PROJ
        o_ref[...] = (acc * jnp.where(is_q, QK_SCALE, 1.0)).astype(BF16)

        @pl.when(jnp.logical_and(is_v, i == 0))
        def _():
            o_ref[0:N_PAD, :] = jnp.zeros((N_PAD, TN_PROJ), BF16)

    @pl.when(j >= N_QKV_TILES)
    def _():
        o_ref[...] = jax.nn.sigmoid(acc).astype(BF16)


def _in_proj(hb, w_in_t, layer):
    l, d = hb.shape
    n_tiles = N_QKV_TILES + 2 * D_MODEL // TN_PROJ
    return pl.pallas_call(
        functools.partial(_in_proj_kernel, layer),
        grid=(n_tiles, l // TM_PROJ),
        in_specs=[
            pl.BlockSpec((TM_PROJ, d), lambda j, i: (i, 0)),
            pl.BlockSpec(memory_space=pl.ANY),
        ],
        out_specs=pl.BlockSpec((TM_PROJ, TN_PROJ), lambda j, i: (i, j)),
        out_shape=jax.ShapeDtypeStruct((l, n_tiles * TN_PROJ), BF16),
        scratch_shapes=[pltpu.VMEM((2, TN_PROJ, d), F32), pltpu.VMEM((TN_PROJ, d), BF16),
                        pltpu.SemaphoreType.DMA((2,))],
        compiler_params=_cparams(("arbitrary", "arbitrary")),
        name="in_proj",
    )(hb, w_in_t)


def _forget_kernel(h_ref, w_ref, b_ref, o_ref, carry_ref):
    i = pl.program_id(0)

    @pl.when(i == 0)
    def _():
        carry_ref[...] = jnp.zeros_like(carry_ref)

    f = lax.dot_general(w_ref[...].astype(BF16), h_ref[...], (((1,), (1,)), ((), ())),
                        preferred_element_type=F32) + b_ref[...]
    pos = i * TM_LN + lax.broadcasted_iota(jnp.int32, f.shape, 1)
    lf = jnp.where(pos >= N_PAD, _log_sigmoid(f), 0.0)
    r = lax.broadcasted_iota(jnp.int32, (TM_LN, TM_LN), 0)
    c = lax.broadcasted_iota(jnp.int32, (TM_LN, TM_LN), 1)
    upto = jnp.where(r <= c, 1.0, 0.0).astype(BF16)
    hi = lf.astype(BF16)
    rest = lf - hi.astype(F32)
    mid = rest.astype(BF16)
    lo = (rest - mid.astype(F32)).astype(BF16)
    prefix = lambda part: jnp.dot(part, upto, preferred_element_type=F32)
    cs = (prefix(lo) + prefix(mid)) + prefix(hi) + carry_ref[:, 0:1]
    o_ref[...] = jnp.where(pos >= N_PAD, -cs, NEG)
    carry_ref[...] = jnp.broadcast_to(cs[:, TM_LN - 1:TM_LN], carry_ref.shape)


def _forget_bias(hb, w_in_t, bf, layer):
    l, d = hb.shape
    return pl.pallas_call(
        _forget_kernel,
        grid=(l // TM_LN,),
        in_specs=[
            pl.BlockSpec((TM_LN, d), lambda i: (i, 0)),
            pl.BlockSpec((None, N_HEADS, d), lambda i: (layer, F_COL0 // N_HEADS, 0)),
            pl.BlockSpec((N_HEADS, TM_LN), lambda i: (0, 0)),
        ],
        out_specs=pl.BlockSpec((N_HEADS, TM_LN), lambda i: (0, i)),
        out_shape=jax.ShapeDtypeStruct((N_HEADS, l), F32),
        scratch_shapes=[pltpu.VMEM((N_HEADS, LANE), F32)],
        compiler_params=_cparams(("arbitrary",)),
        name="forget_bias",
    )(hb, w_in_t, bf)


def _sb_kernel(q_ref, k_ref, v_ref, o_ref, run_ref, acc_ref):
    i = pl.program_id(0)
    tq, tk = SB_TQ, SB_TK
    r = lax.broadcasted_iota(jnp.int32, (2 * tk, 2 * tk), 0)
    c = lax.broadcasted_iota(jnp.int32, (2 * tk, 2 * tk), 1)
    suffix = jnp.where(jnp.logical_or(c >= tk, (r % tk) > c), 1.0, 0.0).astype(BF16)
    row = lax.broadcasted_iota(jnp.int32, (tq, tk), 0)
    col = lax.broadcasted_iota(jnp.int32, (tq, tk), 1)
    strictly_causal = col < row

    def block(j, diagonal):
        start = pl.multiple_of(j * tk, tk)
        heads = range(N_HEADS)
        cols = [slice(hd * HEAD_DIM, (hd + 1) * HEAD_DIM) for hd in heads]
        zs = [lax.dot_general(q_ref[:, c], k_ref[pl.ds(start, tk), c], (((1,), (1,)), ((), ())),
                              preferred_element_type=F32) for c in cols]
        log_betas, cats = [], []
        for z in zs:
            log_beta = _log_sigmoid(z)
            log_keep = log_beta - z
            if diagonal:
                log_keep = jnp.where(strictly_causal, log_keep, 0.0)
            hi = log_keep.astype(BF16)
            lo = (log_keep - hi.astype(F32)).astype(BF16)
            log_betas.append(log_beta)
            cats.append(jnp.concatenate([hi, lo], axis=1))
        sums = [jnp.dot(cat, suffix, preferred_element_type=F32) for cat in cats]
        ws, worst = [], None
        for c, log_beta, s in zip(cols, log_betas, sums):
            if diagonal:
                w = jnp.where(strictly_causal, jnp.exp(log_beta + s[:, :tk]), 0.0)
                run = s[:, tk:]
            else:
                run = run_ref[:, c]
                w = jnp.exp(log_beta + s[:, :tk] + run)
                run = run + s[:, tk:]
            run_ref[:, c] = run
            ws.append(w.astype(BF16))
            worst = run if worst is None else jnp.maximum(worst, run)
        for c, w in zip(cols, ws):
            pv = jnp.dot(w, v_ref[pl.ds(start, tk), c], preferred_element_type=F32)
            if diagonal:
                acc_ref[:, c] = pv
            else:
                acc_ref[:, c] += pv
        return jnp.max(worst)

    worst = block(i, True)

    def cond(s):
        j, worst = s
        return jnp.logical_and(j >= 0, worst > SB_EXIT)

    def body(s):
        j, _ = s
        return j - 1, block(j, False)

    lax.while_loop(cond, body, (i - 1, worst))
    o_ref[...] = acc_ref[...].astype(BF16)


def _sb_attention(proj):
    l = proj.shape[0]
    resident = lambda blk: pl.BlockSpec((l, WIDTH), lambda i: (0, blk), pipeline_mode=pl.Buffered(1))
    return pl.pallas_call(
        _sb_kernel,
        grid=(l // SB_TQ,),
        in_specs=[pl.BlockSpec((SB_TQ, WIDTH), lambda i: (i, 0)), resident(1), resident(2)],
        out_specs=pl.BlockSpec((SB_TQ, WIDTH), lambda i: (i, 0)),
        out_shape=jax.ShapeDtypeStruct((l, WIDTH), BF16),
        scratch_shapes=[pltpu.VMEM((SB_TQ, WIDTH), F32), pltpu.VMEM((SB_TQ, WIDTH), F32)],
        compiler_params=_cparams(("arbitrary",)),
        name="sb_attention",
    )(proj, proj, proj)


def _fox_kernel(q_ref, k_ref, v_ref, b_ref, o_ref, vaug_ref):
    i = pl.program_id(1)
    t = FOX_T

    @pl.when(i == 0)
    def _():
        lane = lax.broadcasted_iota(jnp.int32, (vaug_ref.shape[1], HEAD_DIM), 1)
        ones_col = jnp.where(lane == 0, 1.0, 0.0).astype(BF16)
        for hd in range(FOX_GROUP):
            vaug_ref[hd, :, :HEAD_DIM] = v_ref[:, hd * HEAD_DIM:(hd + 1) * HEAD_DIM]
            vaug_ref[hd, :, HEAD_DIM:] = ones_col

    row = lax.broadcasted_iota(jnp.int32, (t, t), 0)
    col = lax.broadcasted_iota(jnp.int32, (t, t), 1)
    causal = col <= row

    def block(j, state, diagonal):
        start = pl.multiple_of(j * t, t)
        heads = range(FOX_GROUP)
        cols = [slice(hd * HEAD_DIM, (hd + 1) * HEAD_DIM) for hd in heads]
        scores = [lax.dot_general(q_ref[:, c], k_ref[pl.ds(start, t), c], (((1,), (1,)), ((), ())),
                                  preferred_element_type=F32) for c in cols]
        ps, scales, ms = [], [], []
        for hd, s in zip(heads, scores):
            m = state[hd][0]
            s = s + b_ref[hd, j]
            if diagonal:
                s = jnp.where(causal, s, NEG)
            m_new = jnp.maximum(m, jnp.max(s, axis=1, keepdims=True))
            ps.append(jnp.exp(s - m_new).astype(BF16))
            scales.append(jnp.exp(m - m_new))
            ms.append(m_new)
        out = []
        for hd in heads:
            pv = jnp.dot(ps[hd], vaug_ref[hd, pl.ds(start, t), :], preferred_element_type=F32)
            out.append((ms[hd], scales[hd] * state[hd][1] + pv))
        return tuple(out)

    init = tuple((jnp.full((t, 1), NEG, F32), jnp.zeros((t, 2 * HEAD_DIM), F32))
                 for _ in range(FOX_GROUP))
    state = block(i, init, True)
    state = lax.fori_loop(0, i, lambda j, s: block(j, s, False), state)
    for hd, (_, acc) in enumerate(state):
        o_ref[:, hd * HEAD_DIM:(hd + 1) * HEAD_DIM] = (
            acc[:, :HEAD_DIM] / acc[:, HEAD_DIM:HEAD_DIM + 1]).astype(BF16)


def _fox_attention(proj, key_bias):
    l = proj.shape[0]
    gw = FOX_GROUP * HEAD_DIM
    first = 3 * WIDTH // gw
    per = WIDTH // gw
    nkb = l // FOX_T
    return pl.pallas_call(
        _fox_kernel,
        grid=(N_HEADS // FOX_GROUP, l // FOX_T),
        in_specs=[
            pl.BlockSpec((FOX_T, gw), lambda g, i: (i, first + g)),
            pl.BlockSpec((l, gw), lambda g, i: (0, first + per + g), pipeline_mode=pl.Buffered(1)),
            pl.BlockSpec((l, gw), lambda g, i: (0, first + 2 * per + g), pipeline_mode=pl.Buffered(1)),
            pl.BlockSpec((FOX_GROUP, nkb, 1, FOX_T), lambda g, i: (g, 0, 0, 0)),
        ],
        out_specs=pl.BlockSpec((FOX_T, gw), lambda g, i: (i, g)),
        out_shape=jax.ShapeDtypeStruct((l, WIDTH), BF16),
        scratch_shapes=[pltpu.VMEM((FOX_GROUP, l, 2 * HEAD_DIM), BF16)],
        compiler_params=_cparams(("arbitrary", "arbitrary")),
        name="fox_attention",
    )(proj, proj, proj, key_bias)


def _merge_kernel(ys_ref, yf_ref, gs_ref, gf_ref, ws_ref, wf_ref, o_ref, wsb_ref, wfb_ref):
    i = pl.program_id(1)

    @pl.when(i == 0)
    def _():
        wsb_ref[...] = ws_ref[...].astype(BF16)
        wfb_ref[...] = wf_ref[...].astype(BF16)

    a = jnp.dot(ys_ref[...], wsb_ref[...], preferred_element_type=F32)
    b = jnp.dot(yf_ref[...], wfb_ref[...], preferred_element_type=F32)
    o_ref[...] = (gs_ref[...].astype(F32) * a + gf_ref[...].astype(F32) * b).astype(BF16)


def _merge(y_sb, y_fx, proj, w_bsb, w_bfx, layer):
    l = y_sb.shape[0]
    d = D_MODEL
    nj = d // TN_PROJ
    act = pl.BlockSpec((TM_PROJ, WIDTH), lambda j, i: (i, 0))
    wspec = pl.BlockSpec((None, WIDTH, TN_PROJ), lambda j, i: (layer, 0, j))
    return pl.pallas_call(
        _merge_kernel,
        grid=(nj, l // TM_PROJ),
        in_specs=[
            act, act,
            pl.BlockSpec((TM_PROJ, TN_PROJ), lambda j, i: (i, N_QKV_TILES + j)),
            pl.BlockSpec((TM_PROJ, TN_PROJ), lambda j, i: (i, N_QKV_TILES + nj + j)),
            wspec, wspec,
        ],
        out_specs=pl.BlockSpec((TM_PROJ, TN_PROJ), lambda j, i: (i, j)),
        out_shape=jax.ShapeDtypeStruct((l, d), BF16),
        scratch_shapes=[pltpu.VMEM((WIDTH, TN_PROJ), BF16), pltpu.VMEM((WIDTH, TN_PROJ), BF16)],
        compiler_params=_cparams(("arbitrary", "arbitrary")),
        name="merge",
    )(y_sb, y_fx, proj, proj, w_bsb, w_bfx)


def _pick(index, values):
    out = values[0]
    for k in range(1, len(values)):
        out = jnp.where(index == k, values[k], out)
    return out


def _grouped_top2(sel, aff):
    srow = [sel[e:e + 1, :] for e in range(N_EXPERTS)]
    arow = [aff[e:e + 1, :] for e in range(N_EXPERTS)]
    n = EXPERTS_PER_GROUP
    score = []
    for g in range(N_GROUPS):
        v = srow[g * n:(g + 1) * n]
        pair = [v[a] + v[b] for a in range(n) for b in range(a + 1, n)]
        score.append(functools.reduce(jnp.maximum, pair))
    group = jnp.zeros_like(score[0], dtype=jnp.int32)
    best = score[0]
    for g in range(1, N_GROUPS):
        better = score[g] > best
        group = jnp.where(better, g, group)
        best = jnp.where(better, score[g], best)
    v = [_pick(group, [srow[g * n + k] for g in range(N_GROUPS)]) for k in range(n)]
    a = [_pick(group, [arow[g * n + k] for g in range(N_GROUPS)]) for k in range(n)]
    i1 = jnp.zeros_like(group)
    m1 = v[0]
    for k in range(1, n):
        better = v[k] > m1
        i1 = jnp.where(better, k, i1)
        m1 = jnp.where(better, v[k], m1)
    first_is_0 = i1 == 0
    i2 = jnp.where(first_is_0, 1, 0)
    m2 = jnp.where(first_is_0, v[1], v[0])
    for k in range(1, n):
        better = jnp.logical_and(i1 != k, v[k] > m2)
        i2 = jnp.where(better, k, i2)
        m2 = jnp.where(better, v[k], m2)
    a1 = _pick(i1, a)
    a2 = _pick(i2, a)
    tot = a1 + a2
    return group * n + i1, group * n + i2, a1 / tot, a2 / tot


def _out_ln_kernel(layer, x_ref, w_hbm, h_ref, g_ref, b_ref, wr_ref, rb_ref,
                   of_ref, ob_ref, hp_ref, ri_ref, rw_ref, cnt_ref, stage_ref, wb_ref, sem):
    i = pl.program_id(0)

    @pl.when(i == 0)
    def _():
        cnt_ref[...] = jnp.zeros_like(cnt_ref)
        n_chunks = D_MODEL // W_CHUNK

        def w_copy(c):
            return pltpu.make_async_copy(w_hbm.at[layer, pl.ds(c * W_CHUNK, W_CHUNK), :],
                                         stage_ref.at[c % 2], sem.at[c % 2])

        w_copy(0).start()
        for c in range(n_chunks):
            if c + 1 < n_chunks:
                w_copy(c + 1).start()
            w_copy(c).wait()
            wb_ref[c * W_CHUNK:(c + 1) * W_CHUNK, :] = stage_ref[c % 2].astype(BF16)

    mix = jnp.dot(x_ref[...], wb_ref[...], preferred_element_type=F32)

    y = _layer_norm(ALPHA * h_ref[...] + mix, g_ref[...], b_ref[...])
    of_ref[...] = y
    ob_ref[...] = y.astype(BF16)
    hp_ref[...] = _pack_bf16_pairs(y)

    logits = lax.dot_general(wr_ref[...], y, (((1,), (1,)), ((), ())),
                             precision=lax.Precision.HIGHEST, preferred_element_type=F32)
    aff = jax.nn.sigmoid(logits)
    e1, e2, w1, w2 = _grouped_top2(aff + rb_ref[...], aff)
    t = TM_LN
    eid = lax.broadcasted_iota(jnp.int32, (N_EXPERTS, t), 0)
    hit1 = eid == e1
    hit2 = eid == e2
    onehot = jnp.where(jnp.logical_or(hit1, hit2), 1.0, 0.0)
    r = lax.broadcasted_iota(jnp.int32, (t, t), 0)
    c = lax.broadcasted_iota(jnp.int32, (t, t), 1)
    before = jnp.where(r < c, 1.0, 0.0).astype(BF16)
    seen = jnp.dot(onehot.astype(BF16), before, preferred_element_type=F32) + cnt_ref[:, 0:1]
    rank1 = jnp.sum(jnp.where(hit1, seen, 0.0), axis=0, keepdims=True)
    rank2 = jnp.sum(jnp.where(hit2, seen, 0.0), axis=0, keepdims=True)
    ri_ref[0:1, :] = e1
    ri_ref[1:2, :] = e2
    ri_ref[2:3, :] = rank1.astype(jnp.int32)
    ri_ref[3:4, :] = rank2.astype(jnp.int32)
    ri_ref[4:8, :] = jnp.zeros((4, t), jnp.int32)
    rw_ref[0:1, :] = w1
    rw_ref[1:2, :] = w2
    rw_ref[2:8, :] = jnp.zeros((6, t), F32)
    cnt_ref[...] += jnp.sum(onehot, axis=1, keepdims=True)


def _out_ln(merged, w_out, hf, g, b, wr_t, rb, layer):
    l, d = hf.shape
    row = pl.BlockSpec((TM_LN, d), lambda i: (i, 0))
    vec = pl.BlockSpec((None, 1, d), lambda i: (layer, 0, 0))
    tok = pl.BlockSpec((8, TM_LN), lambda i: (0, i))
    return pl.pallas_call(
        functools.partial(_out_ln_kernel, layer),
        grid=(l // TM_LN,),
        in_specs=[
            row,
            pl.BlockSpec(memory_space=pl.ANY),
            row, vec, vec,
            pl.BlockSpec((N_EXPERTS, d), lambda i: (0, 0)),
            pl.BlockSpec((N_EXPERTS, TM_LN), lambda i: (0, 0)),
        ],
        out_specs=[row, row, pl.BlockSpec((TM_LN, d // 2), lambda i: (i, 0)), tok, tok,
                   pl.BlockSpec((N_EXPERTS, LANE), lambda i: (0, 0))],
        out_shape=[jax.ShapeDtypeStruct((l, d), F32), jax.ShapeDtypeStruct((l, d), BF16),
                   jax.ShapeDtypeStruct((l, d // 2), jnp.uint32),
                   jax.ShapeDtypeStruct((8, l), jnp.int32), jax.ShapeDtypeStruct((8, l), F32),
                   jax.ShapeDtypeStruct((N_EXPERTS, LANE), F32)],
        scratch_shapes=[pltpu.VMEM((2, W_CHUNK, d), F32), pltpu.VMEM((d, d), BF16),
                        pltpu.SemaphoreType.DMA((2,))],
        compiler_params=_cparams(("arbitrary",)),
        name="out_ln",
    )(merged, w_out, hf, g.reshape(DEPTH, 1, d), b.reshape(DEPTH, 1, d), wr_t, rb)


def _invert_kernel(pos1_ref, pos2_ref, tok_ref):
    def clear(r, carry):
        tok_ref[r] = 0
        return carry

    lax.fori_loop(0, tok_ref.shape[0], clear, 0, unroll=8)

    def mark(t, carry):
        tok_ref[pos1_ref[t]] = t
        tok_ref[pos2_ref[t]] = t
        return carry

    lax.fori_loop(0, pos1_ref.shape[0], mark, 0, unroll=8)


def _invert(pos1, pos2, n_rows):
    smem = pl.BlockSpec(memory_space=pltpu.SMEM)
    return pl.pallas_call(
        _invert_kernel,
        in_specs=[smem, smem],
        out_specs=smem,
        out_shape=jax.ShapeDtypeStruct((n_rows,), jnp.int32),
        name="invert",
    )(pos1, pos2)


def _pack_bf16_pairs(y):
    bits = pltpu.bitcast(y.astype(BF16).astype(F32), jnp.uint32)
    half = y.shape[1] // 2
    return (bits[:, :half] >> 16) | (bits[:, half:] & jnp.uint32(0xFFFF0000))


def _unpack_bf16_pairs(words):
    return (pltpu.bitcast(words << 16, F32), pltpu.bitcast(words & jnp.uint32(0xFFFF0000), F32))


def _expert_kernel(be_ref, nu_ref, nv_ref, tok_ref, hp_hbm, wg_ref, wu_ref, wd_ref, o_ref,
                   xs_ref, xb_ref, acc_ref, wgb_ref, wub_ref, wdb_ref, sem):
    blk = pl.program_id(0)
    f = pl.program_id(1)
    nf = EXPERT_FF // MOE_TF
    used = blk < nu_ref[0]
    slot = blk % 2
    per_step = MOE_CHUNK // nf
    chunks = [slice(c * MOE_CHUNK, (c + 1) * MOE_CHUNK) for c in range(MOE_CHUNKS)]
    here = [nv_ref[blk] > c * MOE_CHUNK for c in range(MOE_CHUNKS)]
    ahead = [nv_ref[blk + 1] > c * MOE_CHUNK for c in range(MOE_CHUNKS)]

    def row_copy(s, r, src_row):
        return pltpu.make_async_copy(hp_hbm.at[pl.ds(src_row, 1)], xs_ref.at[s, pl.ds(r, 1)], sem.at[s])

    def issue_chunk(first_row):
        def issue(r, carry):
            row_copy(0, first_row + r, tok_ref[first_row + r]).start()
            return carry

        lax.fori_loop(0, MOE_CHUNK, issue, 0, unroll=8)

    def wait_chunk():
        def wait(r, carry):
            row_copy(slot, 0, 0).wait()
            return carry

        lax.fori_loop(0, MOE_CHUNK, wait, 0, unroll=8)

    def unpack(rows):
        lo, hi = _unpack_bf16_pairs(xs_ref[slot, rows, :])
        xb_ref[rows, :D_MODEL // 2] = lo.astype(BF16)
        xb_ref[rows, D_MODEL // 2:] = hi.astype(BF16)

    def compute(rows):
        x = xb_ref[rows, :]
        g = jnp.dot(x, wgb_ref[...], preferred_element_type=F32)
        u = jnp.dot(x, wub_ref[...], preferred_element_type=F32)
        a = (g * jax.nn.sigmoid(g) * u).astype(BF16)
        acc_ref[rows, :] += jnp.dot(a, wdb_ref[...], preferred_element_type=F32)

    for c in range(MOE_CHUNKS):
        @pl.when(jnp.logical_and(jnp.logical_and(blk == 0, f == 0), here[c]))
        def _():
            issue_chunk(c * MOE_CHUNK)

    for c in range(MOE_CHUNKS):
        @pl.when(jnp.logical_and(f == 0, here[c]))
        def _():
            wait_chunk()

    for c in range(MOE_CHUNKS):
        @pl.when(jnp.logical_and(f == 0, here[c]))
        def _():
            unpack(chunks[c])

    @pl.when(jnp.logical_and(used, f == 0))
    def _():
        acc_ref[...] = jnp.zeros_like(acc_ref)

    @pl.when(used)
    def _():
        wgb_ref[...] = wg_ref[...].astype(BF16)
        wub_ref[...] = wu_ref[...].astype(BF16)
        wdb_ref[...] = wd_ref[...].astype(BF16)

    for c in range(MOE_CHUNKS):
        @pl.when(jnp.logical_and(used, ahead[c]))
        def _():
            nxt = (blk + 1) * MOE_TM
            for r in range(per_step):
                row = c * MOE_CHUNK + f * per_step + r
                row_copy(1 - slot, row, tok_ref[nxt + row]).start(priority=r % 2)

        @pl.when(here[c])
        def _():
            compute(chunks[c])

    @pl.when(jnp.logical_and(f == nf - 1, used))
    def _():
        o_ref[...] = _pack_bf16_pairs(acc_ref[...])

    @pl.when(jnp.logical_and(f == nf - 1, jnp.logical_not(used)))
    def _():
        o_ref[...] = jnp.zeros_like(o_ref)


def _expert_ffn(block_e, n_used, n_valid, row_tok, hp, w_gate, w_up, w_down, layer):
    d = D_MODEL
    nb = block_e.shape[0]
    nf = EXPERT_FF // MOE_TF

    def w_in_map(b, f, be, nu, nv, tok):
        return (layer, be[b], 0, jnp.where(b < nu[0], f, nf - 1))

    def w_out_map(b, f, be, nu, nv, tok):
        return (layer, be[b], jnp.where(b < nu[0], f, nf - 1), 0)

    grid_spec = pltpu.PrefetchScalarGridSpec(
        num_scalar_prefetch=4,
        grid=(nb, nf),
        in_specs=[
            pl.BlockSpec(memory_space=pl.ANY),
            pl.BlockSpec((None, None, d, MOE_TF), w_in_map),
            pl.BlockSpec((None, None, d, MOE_TF), w_in_map),
            pl.BlockSpec((None, None, MOE_TF, d), w_out_map),
        ],
        out_specs=pl.BlockSpec((MOE_TM, d // 2), lambda b, f, be, nu, nv, tok: (b, 0)),
        scratch_shapes=[pltpu.VMEM((2, MOE_TM, d // 2), jnp.uint32), pltpu.VMEM((MOE_TM, d), BF16),
                        pltpu.VMEM((MOE_TM, d), F32), pltpu.VMEM((d, MOE_TF), BF16),
                        pltpu.VMEM((d, MOE_TF), BF16), pltpu.VMEM((MOE_TF, d), BF16),
                        pltpu.SemaphoreType.DMA((2,))],
    )
    return pl.pallas_call(
        _expert_kernel,
        grid_spec=grid_spec,
        out_shape=jax.ShapeDtypeStruct((nb * MOE_TM, d // 2), jnp.uint32),
        compiler_params=_cparams(("arbitrary", "arbitrary")),
        name="expert_ffn",
    )(block_e, n_used, n_valid, row_tok, hp, w_gate, w_up, w_down)


def _combine_kernel(first_row, with_bf16, pos1_ref, pos2_ref, yb_hbm, h_ref, w_ref, g_ref, b_ref, *rest):
    out_refs, (y1_ref, y2_ref, sem) = rest[:-3], rest[-3:]
    i = pl.program_id(0)
    slot = i % 2

    def row_copy(dst, s, r, src_row):
        return pltpu.make_async_copy(yb_hbm.at[pl.ds(src_row, 1)], dst.at[s, pl.ds(r, 1)], sem.at[s])

    def issue_tile(tile, s):
        base = first_row + tile * CMB_TM

        def issue(r, carry):
            row_copy(y1_ref, s, r, pos1_ref[base + r]).start(priority=0)
            row_copy(y2_ref, s, r, pos2_ref[base + r]).start(priority=1)
            return carry

        lax.fori_loop(0, CMB_TM, issue, 0, unroll=8)

    @pl.when(i == 0)
    def _():
        issue_tile(0, 0)

    @pl.when(i + 1 < pl.num_programs(0))
    def _():
        issue_tile(i + 1, 1 - slot)

    def wait(r, carry):
        row_copy(y1_ref, slot, 0, 0).wait()
        row_copy(y2_ref, slot, 0, 0).wait()
        return carry

    lax.fori_loop(0, CMB_TM, wait, 0, unroll=8)
    w = w_ref[...]
    lo1, hi1 = _unpack_bf16_pairs(y1_ref[slot])
    lo2, hi2 = _unpack_bf16_pairs(y2_ref[slot])
    y = jnp.concatenate([w[:, 0:1] * lo1 + w[:, 1:2] * lo2, w[:, 0:1] * hi1 + w[:, 1:2] * hi2], axis=1)
    out = _layer_norm(ALPHA * h_ref[...] + y, g_ref[...], b_ref[...])
    out_refs[0][...] = out
    if with_bf16:
        out_refs[1][...] = out.astype(BF16)


def _combine_ln(pos1, pos2, yb, hf, wts, g, b, layer, final):
    l, d = hf.shape
    first_block = PREFIX // CMB_TM if final else 0
    n_rows = l - first_block * CMB_TM
    src = lambda i, p1, p2: (i + first_block, 0)
    dst = lambda i, p1, p2: (i, 0)
    vec = pl.BlockSpec((None, 1, d), lambda i, p1, p2: (layer, 0, 0))
    out_specs = [pl.BlockSpec((CMB_TM, d), dst)]
    out_shape = [jax.ShapeDtypeStruct((n_rows, d), F32)]
    if not final:
        out_specs.append(pl.BlockSpec((CMB_TM, d), dst))
        out_shape.append(jax.ShapeDtypeStruct((n_rows, d), BF16))
    grid_spec = pltpu.PrefetchScalarGridSpec(
        num_scalar_prefetch=2,
        grid=(n_rows // CMB_TM,),
        in_specs=[
            pl.BlockSpec(memory_space=pl.ANY),
            pl.BlockSpec((CMB_TM, d), src),
            pl.BlockSpec((CMB_TM, TOP_K), src),
            vec, vec,
        ],
        out_specs=out_specs,
        scratch_shapes=[
            pltpu.VMEM((2, CMB_TM, d // 2), jnp.uint32),
            pltpu.VMEM((2, CMB_TM, d // 2), jnp.uint32),
            pltpu.SemaphoreType.DMA((2,)),
        ],
    )
    return pl.pallas_call(
        functools.partial(_combine_kernel, first_block * CMB_TM, not final),
        grid_spec=grid_spec,
        out_shape=out_shape,
        compiler_params=_cparams(("arbitrary",)),
        name="combine_ln",
    )(pos1, pos2, yb, hf, wts, g.reshape(DEPTH, 1, d), b.reshape(DEPTH, 1, d))


def _plan(route_i, counts_f):
    n = route_i.shape[1]
    nb = -(-(n * TOP_K + N_EXPERTS * (MOE_TM - 1)) // MOE_TM)
    counts = counts_f[:, 0].astype(jnp.int32)
    blocks_e = (counts + MOE_TM - 1) // MOE_TM
    bend = jnp.cumsum(blocks_e)
    base = (bend - blocks_e) * MOE_TM
    eids = jnp.arange(N_EXPERTS, dtype=jnp.int32)[:, None]
    base_of = lambda e: jnp.sum(jnp.where(e[None, :] == eids, base[:, None], 0), axis=0)
    pos1 = (route_i[2] + base_of(route_i[0])).astype(jnp.int32)
    pos2 = (route_i[3] + base_of(route_i[1])).astype(jnp.int32)
    n_used = bend[-1:].astype(jnp.int32)
    blocks = jnp.arange(nb, dtype=jnp.int32)
    block_e = jnp.minimum(jnp.searchsorted(bend, blocks, side='right'), N_EXPERTS - 1).astype(jnp.int32)
    done = (blocks - (bend - blocks_e)[block_e]) * MOE_TM
    n_valid = jnp.where(blocks < n_used[0], jnp.clip(counts[block_e] - done, 0, MOE_TM), 0)
    n_valid = jnp.concatenate([n_valid, jnp.zeros((1,), jnp.int32)]).astype(jnp.int32)
    return pos1, pos2, block_e, n_used, n_valid


def kernel(x, meta_tokens, ln_in_g, ln_in_b, w_in, b_forget, w_branch_sb, w_branch_fox, w_out,
           ln_mix_g, ln_mix_b, w_router, router_bias, w_gate, w_up, w_down, ln_ffn_g, ln_ffn_b):
    b, s, d = x.shape
    assert b == 1 and d == D_MODEL
    l = s + PREFIX
    prefix = jnp.concatenate([jnp.zeros((N_PAD, d), x.dtype), meta_tokens.astype(x.dtype)], axis=0)
    hf, hb = _ln_in(prefix, x[0], ln_in_g, ln_in_b)

    w_in_t = jnp.swapaxes(w_in, 1, 2)
    wr_t = w_router.astype(F32).T
    rb = jnp.broadcast_to(router_bias.astype(F32)[:, None], (N_EXPERTS, TM_LN))
    for layer in range(DEPTH):
        proj = _in_proj(hb, w_in_t, layer)
        bf = jnp.broadcast_to(b_forget[layer].astype(F32)[:, None], (N_HEADS, TM_LN))
        key_bias = _forget_bias(hb, w_in_t, bf, layer).reshape(N_HEADS, l // FOX_T, 1, FOX_T)
        y_sb = _sb_attention(proj)
        y_fx = _fox_attention(proj, key_bias)
        merged = _merge(y_sb, y_fx, proj, w_branch_sb, w_branch_fox, layer)
        hf, hb, hp, route_i, route_w, counts = _out_ln(merged, w_out, hf, ln_mix_g, ln_mix_b,
                                                       wr_t, rb, layer)

        pos1, pos2, block_e, n_used, n_valid = _plan(route_i, counts)
        row_tok = _invert(pos1, pos2, block_e.shape[0] * MOE_TM)
        yb = _expert_ffn(block_e, n_used, n_valid, row_tok, hp, w_gate, w_up, w_down, layer)
        final = layer == DEPTH - 1
        outs = _combine_ln(pos1, pos2, yb, hf, route_w[:TOP_K].T, ln_ffn_g, ln_ffn_b, layer, final)
        if not final:
            hf, hb = outs
    return outs[0][None]
```
